```python
import jax, jax.numpy as jnp
from jax import lax
import numpy as np

D_MODEL = 4096
BATCH = 2
SEQ = 8192
DEPTH = 1
DEC_BATCH = 32
DEC_SEQ = 64
PAST_LEN = 4096

CHUNK = 64
GLA_BLOCK = CHUNK // 4
GLA_HEADS = 4
GLA_K = D_MODEL // 4
GLA_V = D_MODEL // 2
GLA_DK = GLA_K // GLA_HEADS
GLA_DV = GLA_V // GLA_HEADS
GATE_RANK = 16
GATE_NORM = 16.0
CONV_W = D_MODEL // 4
CONV_K = 3
XA_HEADS = 4
XA_W = D_MODEL // 4
XA_DH = XA_W // XA_HEADS
N_MEM = 256
D_FF = ((8 * D_MODEL + 3 * 256 - 1) // (3 * 256)) * 256
N_BRANCH = 3
IN_COLS = 2 * GLA_K + 2 * GLA_V + GATE_RANK + 3 * CONV_W + XA_W + N_BRANCH * D_MODEL
EPS = 1e-6

kernel_name = "hybrid_gla_shortconv_memxattn_stream_step"


def _rmsnorm(x, g):
    xf = x.astype(jnp.float32)
    y = xf * lax.rsqrt(jnp.mean(xf * xf, axis=-1, keepdims=True) + EPS)
    return (y * g.astype(jnp.float32)).astype(x.dtype)


def _gla(q, k, v, log_a, s0):
    B, T, H, _ = q.shape
    DV = v.shape[-1]
    pad = (-T) % GLA_BLOCK
    nb = (T + pad) // GLA_BLOCK

    def blk(a):
        a = jnp.pad(a.astype(jnp.float32), ((0, 0), (0, pad), (0, 0), (0, 0)))
        return jnp.moveaxis(a.reshape(B, nb, GLA_BLOCK, H, a.shape[-1]), 1, 0)

    qb, kb, vb, gb = blk(q), blk(k), blk(v), blk(log_a)
    b = jnp.cumsum(gb, axis=2)
    b_last = b[:, :, -1]
    q_in = qb * jnp.exp(b)
    k_in = kb * jnp.exp(-b)
    k_out = kb * jnp.exp(b_last[:, :, None] - b)
    mask = jnp.tril(jnp.ones((GLA_BLOCK, GLA_BLOCK), dtype=bool))
    scores = jnp.where(mask, jnp.einsum('nbthk,nbshk->nbhts', q_in, k_in), 0.0)
    o_intra = jnp.einsum('nbhts,nbshv->nbthv', scores, vb)

    def step(S, inp):
        qi, ko, vv, bl = inp
        o = jnp.einsum('bthk,bhkv->bthv', qi, S)
        S = jnp.exp(bl)[..., None] * S + jnp.einsum('bshk,bshv->bhkv', ko, vv)
        return S, o

    S, o_inter = lax.scan(step, s0.astype(jnp.float32), (q_in, k_out, vb, b_last))
    o = jnp.moveaxis(o_intra + o_inter, 0, 1).reshape(B, T + pad, H, DV)[:, :T]
    return o.astype(v.dtype), S


def _mem_kv(mem, g_mem, w_mem_kv):
    B, M, _ = mem.shape
    k, v = jnp.split(_rmsnorm(mem, g_mem) @ w_mem_kv, 2, axis=-1)
    return k.reshape(B, M, XA_HEADS, XA_DH), v.reshape(B, M, XA_HEADS, XA_DH)


def _layer(x, s0, buf, mem_k, mem_v, g_mix, w_in, w_a2, b_a, g_gla_out, w_gla_o, w_conv, w_conv_o,
           w_xa_o, b_merge, w_out, g_ffn, w_ffn_gate, w_ffn_up, w_ffn_down):
    B, T, _ = x.shape
    xn = _rmsnorm(x, g_mix)
    z = xn @ w_in
    sizes = (GLA_K, GLA_K, GLA_V, GATE_RANK, GLA_V, CONV_W, CONV_W, CONV_W, XA_W, D_MODEL, D_MODEL, D_MODEL)
    q, k, v, a_lr, r, cb, cc, ch, xq, m0, m1, m2 = jnp.split(z, list(np.cumsum(sizes)[:-1]), axis=-1)

    q = q.reshape(B, T, GLA_HEADS, GLA_DK) * (GLA_DK ** -0.5)
    k = k.reshape(B, T, GLA_HEADS, GLA_DK)
    v = v.reshape(B, T, GLA_HEADS, GLA_DV)
    log_a = (jax.nn.log_sigmoid((a_lr @ w_a2 + b_a).astype(jnp.float32)) / GATE_NORM).reshape(B, T, GLA_HEADS, GLA_DK)
    o, s_new = _gla(q, k, v, log_a, s0)
    o = _rmsnorm(o, g_gla_out).reshape(B, T, GLA_V) * jax.nn.silu(r)
    y_gla = o @ w_gla_o

    u = cc * ch
    full = jnp.concatenate([buf.astype(u.dtype), u], axis=1)
    conv = sum(full[:, j:j + T] * w_conv[j] for j in range(CONV_K))
    new_buf = full[:, -(CONV_K - 1):]
    y_conv = (cb * conv) @ w_conv_o

    xq = xq.reshape(B, T, XA_HEADS, XA_DH)
    s = jnp.einsum('bthd,bmhd->bhtm', xq, mem_k).astype(jnp.float32) * (XA_DH ** -0.5)
    p = jax.nn.softmax(s, axis=-1).astype(x.dtype)
    y_xa = jnp.einsum('bhtm,bmhd->bthd', p, mem_v).reshape(B, T, XA_W) @ w_xa_o

    mixed = (jax.nn.sigmoid(m0 + b_merge[0]) * y_gla + jax.nn.sigmoid(m1 + b_merge[1]) * y_conv
             + jax.nn.sigmoid(m2 + b_merge[2]) * y_xa)
    x = x + mixed @ w_out

    hn = _rmsnorm(x, g_ffn)
    x = x + (jax.nn.silu(hn @ w_ffn_gate) * (hn @ w_ffn_up)) @ w_ffn_down
    return x, s_new.astype(s0.dtype), new_buf.astype(buf.dtype)


def setup_inputs(seed: int = 0) -> dict:
    key = jax.random.key(seed)
    ks = jax.random.split(key, 32)
    f32 = jnp.float32
    nrm = lambda k, shape, s: jax.random.normal(k, shape, f32) * s
    gain = lambda k, shape: 1.0 + 0.02 * jax.random.normal(k, shape, f32)
    L = DEPTH
    return {
        "x_prompt": nrm(ks[0], (BATCH, SEQ, D_MODEL), 1.0),
        "x_sample": nrm(ks[1], (DEC_BATCH, DEC_SEQ, D_MODEL), 1.0),
        "state_gla": nrm(ks[2], (L, DEC_BATCH, GLA_HEADS, GLA_DK, GLA_DV), 1.0),
        "cache_conv": nrm(ks[3], (L, DEC_BATCH, CONV_K - 1, CONV_W), 1.0),
        "cache_mem_k": nrm(ks[4], (L, DEC_BATCH, N_MEM, XA_HEADS, XA_DH), 1.0),
        "cache_mem_v": nrm(ks[5], (L, DEC_BATCH, N_MEM, XA_HEADS, XA_DH), 1.0),
        "mem_prompt": nrm(ks[6], (BATCH, N_MEM, D_MODEL), 1.0),
        "g_mix": gain(ks[7], (L, D_MODEL)),
        "w_in": nrm(ks[8], (L, D_MODEL, IN_COLS), D_MODEL ** -0.5),
        "w_a2": nrm(ks[9], (L, GATE_RANK, GLA_K), GATE_RANK ** -0.5),
        "b_a": nrm(ks[10], (L, GLA_K), 0.1),
        "g_gla_out": gain(ks[11], (L, GLA_DV)),
        "w_gla_o": nrm(ks[12], (L, GLA_V, D_MODEL), GLA_V ** -0.5),
        "w_conv": nrm(ks[13], (L, CONV_K, CONV_W), CONV_K ** -0.5),
        "w_conv_o": nrm(ks[14], (L, CONV_W, D_MODEL), CONV_W ** -0.5),
        "w_xa_o": nrm(ks[15], (L, XA_W, D_MODEL), XA_W ** -0.5),
        "g_mem": gain(ks[16], (L, D_MODEL)),
        "w_mem_kv": nrm(ks[17], (L, D_MODEL, 2 * XA_W), D_MODEL ** -0.5),
        "b_merge": nrm(ks[18], (L, N_BRANCH, D_MODEL), 0.1),
        "w_out": nrm(ks[19], (L, D_MODEL, D_MODEL), D_MODEL ** -0.5),
        "g_ffn": gain(ks[20], (L, D_MODEL)),
        "w_ffn_gate": nrm(ks[21], (L, D_MODEL, D_FF), D_MODEL ** -0.5),
        "w_ffn_up": nrm(ks[22], (L, D_MODEL, D_FF), D_MODEL ** -0.5),
        "w_ffn_down": nrm(ks[23], (L, D_FF, D_MODEL), D_FF ** -0.5),
        "g_final": gain(ks[24], (D_MODEL,)),
    }


def reference(x_prompt, x_sample, state_gla, cache_conv, cache_mem_k, cache_mem_v, mem_prompt,
              g_mix, w_in, w_a2, b_a, g_gla_out, w_gla_o, w_conv, w_conv_o, w_xa_o, g_mem, w_mem_kv,
              b_merge, w_out, g_ffn, w_ffn_gate, w_ffn_up, w_ffn_down, g_final):
    bp = x_prompt.shape[0]
    hp, hs = x_prompt, x_sample
    gla_p, conv_p, mk_p, mv_p, gla_s, conv_s = [], [], [], [], [], []
    for l in range(DEPTH):
        lw = (g_mix[l], w_in[l], w_a2[l], b_a[l], g_gla_out[l], w_gla_o[l], w_conv[l], w_conv_o[l],
              w_xa_o[l], b_merge[l], w_out[l], g_ffn[l], w_ffn_gate[l], w_ffn_up[l], w_ffn_down[l])
        mk, mv = _mem_kv(mem_prompt, g_mem[l], w_mem_kv[l])
        s0 = jnp.zeros((bp, GLA_HEADS, GLA_DK, GLA_DV), state_gla.dtype)
        buf0 = jnp.zeros((bp, CONV_K - 1, CONV_W), cache_conv.dtype)
        hp, sp, bufp = _layer(hp, s0, buf0, mk, mv, *lw)
        hs, ss, bufs = _layer(hs, state_gla[l], cache_conv[l], cache_mem_k[l], cache_mem_v[l], *lw)
        gla_p.append(sp); conv_p.append(bufp); mk_p.append(mk); mv_p.append(mv)
        gla_s.append(ss); conv_s.append(bufs)
    y_prompt = _rmsnorm(hp, g_final)
    y_sample = _rmsnorm(hs, g_final)
    return (y_prompt, y_sample, jnp.stack(gla_p), jnp.stack(conv_p), jnp.stack(mk_p), jnp.stack(mv_p),
            jnp.stack(gla_s), jnp.stack(conv_s))
```

```python
import functools

import jax
import jax.numpy as jnp
from jax import lax
from jax.experimental import pallas as pl
from jax.experimental.pallas import tpu as pltpu

F32 = jnp.float32
BF16 = jnp.bfloat16

EPS = 1e-6
GLA_BLOCK = 16
GATE_NORM = 16.0
LANES = 128
VMEM_LIMIT = 56 * 1024 * 1024


def _cparams(*sem):
    return pltpu.CompilerParams(dimension_semantics=sem, vmem_limit_bytes=VMEM_LIMIT)


def _tile(n, pref):
    if n <= pref:
        return n
    t = pref
    while n % t:
        t //= 2
    return t


def _rms(x, g):
    ms = jnp.mean(x * x, axis=-1, keepdims=True)
    return x * lax.rsqrt(ms + EPS) * g


def _dot(a, b):
    return jnp.dot(a, b, preferred_element_type=F32)


def _dot_nt(a, b):
    return lax.dot_general(a, b, (((1,), (1,)), ((), ())), preferred_element_type=F32)


def _dot_tn(a, b):
    return lax.dot_general(a, b, (((0,), (0,)), ((), ())), preferred_element_type=F32)


def _rows_loop(n_rows, rb, fn):
    rb = min(rb, n_rows)

    def body(i, c):
        fn(pl.ds(pl.multiple_of(i * rb, rb), rb))
        return c

    lax.fori_loop(0, n_rows // rb, body, 0)


def _inproj_kernel(x_ref, g_ref, w_ref, wa_ref, z_ref, a_ref, xn_ref):
    @pl.when(pl.program_id(1) == 0)
    def _():
        g = g_ref[...]

        def norm(sl):
            xn_ref[sl, :] = _rms(x_ref[sl, :], g).astype(BF16)

        _rows_loop(x_ref.shape[0], 256, norm)
        a_ref[...] = _dot(xn_ref[...], wa_ref[...])

    z_ref[...] = _dot(xn_ref[...], w_ref[...]).astype(z_ref.dtype)


def _inproj(x, g, w, wa, *, tm, tn):
    m, d = x.shape
    n = w.shape[1]
    tm, tn = _tile(m, tm), _tile(n, tn)
    return pl.pallas_call(
        _inproj_kernel,
        grid=(m // tm, n // tn),
        in_specs=[
            pl.BlockSpec((tm, d), lambda i, j: (i, 0), pipeline_mode=pl.Buffered(1)),
            pl.BlockSpec((1, d), lambda i, j: (0, 0)),
            pl.BlockSpec((d, tn), lambda i, j: (0, j)),
            pl.BlockSpec((d, LANES), lambda i, j: (0, 0)),
        ],
        out_specs=[
            pl.BlockSpec((tm, tn), lambda i, j: (i, j)),
            pl.BlockSpec((tm, LANES), lambda i, j: (i, 0)),
        ],
        out_shape=[jax.ShapeDtypeStruct((m, n), BF16), jax.ShapeDtypeStruct((m, LANES), F32)],
        scratch_shapes=[pltpu.VMEM((tm, d), BF16)],
        compiler_params=_cparams("parallel", "arbitrary"),
        name="inproj",
    )(x, g, w, wa)


def _norm_matmul_kernel(x_ref, g_ref, w_ref, o_ref, xn_ref):
    @pl.when(pl.program_id(1) == 0)
    def _():
        g = g_ref[...]

        def norm(sl):
            xn_ref[sl, :] = _rms(x_ref[sl, :], g).astype(BF16)

        _rows_loop(x_ref.shape[0], 256, norm)

    o_ref[...] = _dot(xn_ref[...], w_ref[...])


def _norm_matmul(x, g, w, *, tm, tn):
    m, d = x.shape
    n = w.shape[1]
    tm, tn = _tile(m, tm), _tile(n, tn)
    return pl.pallas_call(
        _norm_matmul_kernel,
        grid=(m // tm, n // tn),
        in_specs=[
            pl.BlockSpec((tm, d), lambda i, j: (i, 0)),
            pl.BlockSpec((1, d), lambda i, j: (0, 0)),
            pl.BlockSpec((d, tn), lambda i, j: (0, j)),
        ],
        out_specs=pl.BlockSpec((tm, tn), lambda i, j: (i, j)),
        out_shape=jax.ShapeDtypeStruct((m, n), F32),
        scratch_shapes=[pltpu.VMEM((tm, d), BF16)],
        compiler_params=_cparams("parallel", "arbitrary"),
        name="memkv",
    )(x, g, w)


def _cumsum_rows(x, rows):
    s = 1
    while s < x.shape[0]:
        x = x + jnp.where(rows >= s, pltpu.roll(x, s, 0), 0.0)
        s *= 2
    return x


def _group_end(x, h):
    n, w = x.shape
    return jnp.concatenate(
        [jnp.broadcast_to(x[(g + 1) * h - 1:(g + 1) * h, :], (h, w)) for g in range(n // h)], axis=0)


def _gla_kernel(q_ref, k_ref, v_ref, r_ref, a_ref, wa2_ref, ba_ref, g_ref, s0_ref,
                og_ref, sout_ref, s_ref, *, chunk, scale):
    t = pl.program_id(2)

    @pl.when(t == 0)
    def _():
        s_ref[...] = s0_ref[0, 0]

    tc, dk = q_ref.shape
    c = chunk
    rows = lax.broadcasted_iota(jnp.int32, (c, dk), 0)
    ri = lax.broadcasted_iota(jnp.int32, (c, c), 0)
    ci = lax.broadcasted_iota(jnp.int32, (c, c), 1)
    hs = []
    h = GLA_BLOCK
    while h <= c:
        hs.append(h)
        h *= 2
    blk = GLA_BLOCK.bit_length() - 1
    mask0 = (lax.shift_right_logical(ri, blk) == lax.shift_right_logical(ci, blk)) & (ci <= ri)
    masks = {}
    for h in hs[:-1]:
        sh = (2 * h).bit_length() - 1
        same = lax.shift_right_logical(ri, sh) == lax.shift_right_logical(ci, sh)
        masks[h] = same & ((ri & (2 * h - 1)) >= h) & ((ci & (2 * h - 1)) < h)

    wa2 = wa2_ref[...]
    ba = ba_ref[...]
    g = g_ref[...]

    def chunk_body(i, carry):
        sl = pl.ds(pl.multiple_of(i * c, c), c)
        la = jax.nn.log_sigmoid(_dot(a_ref[sl, :].astype(BF16), wa2) + ba) / GATE_NORM
        beta = _cumsum_rows(la, rows)
        q = q_ref[sl, :].astype(F32) * scale
        k = k_ref[sl, :].astype(F32)
        v = v_ref[sl, :]
        a_mat = None
        qc = kc = e_c = None
        for h in hs:
            e_h = _group_end(beta, h)
            if h == c:
                s_h = jnp.zeros_like(beta)
            else:
                s_h = jnp.concatenate([jnp.zeros((h, dk), F32), e_h[:c - h, :]], axis=0)
            qh = (q * jnp.exp(beta - s_h)).astype(BF16)
            kh = (k * jnp.exp(e_h - beta)).astype(BF16)
            if h == GLA_BLOCK:
                k_in = (k * jnp.exp(s_h - beta)).astype(BF16)
                a_mat = jnp.where(mask0, _dot_nt(qh, k_in), 0.0)
            if h < c:
                a_mat = a_mat + jnp.where(masks[h], _dot_nt(qh, kh), 0.0)
            else:
                qc, kc, e_c = qh, kh, e_h
        s_old = s_ref[...]
        o = _dot(a_mat.astype(BF16), v) + _dot(qc, s_old.astype(BF16))
        dec = jnp.transpose(jnp.exp(e_c[0:8, :]))[:, 0:1]
        s_ref[...] = s_old * dec + _dot_tn(kc, v)
        on = _rms(o, g)
        r = r_ref[sl, :].astype(F32)
        og_ref[sl, :] = (on * (r * jax.nn.sigmoid(r))).astype(og_ref.dtype)
        return carry

    lax.fori_loop(0, tc // c, chunk_body, 0)

    @pl.when(t == pl.num_programs(2) - 1)
    def _():
        sout_ref[0, 0] = s_ref[...]


def _gla(z, a_lr, wa2, ba, g, s0, *, batch, seq, heads, dk, dv, off_q, off_k, off_v, off_r, tc, chunk):
    m = z.shape[0]
    tc = _tile(seq, tc)
    chunk = min(chunk, tc)
    nt = seq // tc
    assert off_q % dk == 0 and off_k % dk == 0 and off_v % dv == 0 and off_r % dv == 0
    assert chunk % GLA_BLOCK == 0 and tc % chunk == 0

    def zspec(width, off):
        return pl.BlockSpec((tc, width), lambda b, h, t: (b * nt + t, off // width + h))

    return pl.pallas_call(
        functools.partial(_gla_kernel, chunk=chunk, scale=float(dk) ** -0.5),
        grid=(batch, heads, nt),
        in_specs=[
            zspec(dk, off_q), zspec(dk, off_k), zspec(dv, off_v), zspec(dv, off_r),
            pl.BlockSpec((tc, LANES), lambda b, h, t: (b * nt + t, 0)),
            pl.BlockSpec((LANES, dk), lambda b, h, t: (0, h)),
            pl.BlockSpec((1, dk), lambda b, h, t: (0, h)),
            pl.BlockSpec((1, dv), lambda b, h, t: (0, 0)),
            pl.BlockSpec((1, 1, dk, dv), lambda b, h, t: (b, h, 0, 0)),
        ],
        out_specs=[
            pl.BlockSpec((tc, dv), lambda b, h, t: (b * nt + t, h)),
            pl.BlockSpec((1, 1, dk, dv), lambda b, h, t: (b, h, 0, 0)),
        ],
        out_shape=[jax.ShapeDtypeStruct((m, heads * dv), BF16),
                   jax.ShapeDtypeStruct((batch, heads, dk, dv), F32)],
        scratch_shapes=[pltpu.VMEM((dk, dv), F32)],
        compiler_params=_cparams("parallel", "parallel", "arbitrary"),
        name="gla",
    )(z, z, z, z, a_lr, wa2, ba, g, s0)


def _conv_kernel(cb_ref, cc_ref, ch_ref, buf_ref, w_ref, cg_ref, nb_ref, tail_ref):
    t = pl.program_id(1)

    @pl.when(t == 0)
    def _():
        tail_ref[0:2, :] = buf_ref[0]

    tt = cc_ref.shape[0]
    u = cc_ref[...].astype(F32) * ch_ref[...].astype(F32)
    rows = lax.broadcasted_iota(jnp.int32, u.shape, 0)
    p0 = tail_ref[0:1, :]
    p1 = tail_ref[1:2, :]
    u1 = jnp.where(rows == 0, p1, pltpu.roll(u, 1, 0))
    u2 = jnp.where(rows == 0, p0, jnp.where(rows == 1, p1, pltpu.roll(u, 2, 0)))
    w = w_ref[...]
    conv = u2 * w[0:1, :] + u1 * w[1:2, :] + u * w[2:3, :]
    cg_ref[...] = (cb_ref[...].astype(F32) * conv).astype(cg_ref.dtype)
    last2 = u[tt - 2:tt, :]
    tail_ref[0:2, :] = last2

    @pl.when(t == pl.num_programs(1) - 1)
    def _():
        nb_ref[0] = last2


def _conv(z, buf, w, *, batch, seq, cw, off_cb, tt):
    m = z.shape[0]
    tt = _tile(seq, tt)
    nt = seq // tt
    assert off_cb % cw == 0 and tt >= 2
    kk = buf.shape[1]

    def zspec(k):
        return pl.BlockSpec((tt, cw), lambda b, t: (b * nt + t, off_cb // cw + k))

    return pl.pallas_call(
        _conv_kernel,
        grid=(batch, nt),
        in_specs=[zspec(0), zspec(1), zspec(2),
                  pl.BlockSpec((1, kk, cw), lambda b, t: (b, 0, 0)),
                  pl.BlockSpec((kk + 1, cw), lambda b, t: (0, 0))],
        out_specs=[pl.BlockSpec((tt, cw), lambda b, t: (b * nt + t, 0)),
                   pl.BlockSpec((1, kk, cw), lambda b, t: (b, 0, 0))],
        out_shape=[jax.ShapeDtypeStruct((m, cw), BF16), jax.ShapeDtypeStruct((batch, kk, cw), F32)],
        scratch_shapes=[pltpu.VMEM((8, cw), F32)],
        compiler_params=_cparams("parallel", "arbitrary"),
        name="conv",
    )(z, z, z, buf, w)


def _xattn_kernel(q_ref, k_ref, v_ref, o_ref, *, scale):
    s = _dot_nt(q_ref[...], k_ref[0].astype(BF16)) * scale
    p = jnp.exp(s - jnp.max(s, axis=-1, keepdims=True))
    p = p / jnp.sum(p, axis=-1, keepdims=True)
    o_ref[...] = _dot(p.astype(BF16), v_ref[0].astype(BF16)).astype(o_ref.dtype)


def _xattn(z, mem_k, mem_v, *, batch, seq, heads, dh, off_q, koff, voff, tt):
    m = z.shape[0]
    tt = _tile(seq, tt)
    nt = seq // tt
    nm = mem_k.shape[1]
    assert off_q % dh == 0
    return pl.pallas_call(
        functools.partial(_xattn_kernel, scale=float(dh) ** -0.5),
        grid=(batch, nt, heads),
        in_specs=[
            pl.BlockSpec((tt, dh), lambda b, t, h: (b * nt + t, off_q // dh + h)),
            pl.BlockSpec((1, nm, dh), lambda b, t, h: (b, 0, koff + h)),
            pl.BlockSpec((1, nm, dh), lambda b, t, h: (b, 0, voff + h)),
        ],
        out_specs=pl.BlockSpec((tt, dh), lambda b, t, h: (b * nt + t, h)),
        out_shape=jax.ShapeDtypeStruct((m, heads * dh), BF16),
        compiler_params=_cparams("parallel", "parallel", "parallel"),
        name="xattn",
    )(z, mem_k, mem_v)


def _mix_kernel(og_ref, cg_ref, at_ref, m0_ref, m1_ref, m2_ref, bm_ref, wg_ref, wc_ref, wx_ref, o_ref):
    bm = bm_ref[...]
    yg = _dot(og_ref[...], wg_ref[...])
    yc = _dot(cg_ref[...], wc_ref[...])
    yx = _dot(at_ref[...], wx_ref[...])
    mixed = (jax.nn.sigmoid(m0_ref[...].astype(F32) + bm[0:1, :]) * yg
             + jax.nn.sigmoid(m1_ref[...].astype(F32) + bm[1:2, :]) * yc
             + jax.nn.sigmoid(m2_ref[...].astype(F32) + bm[2:3, :]) * yx)
    o_ref[...] = mixed.astype(o_ref.dtype)


def _mix(og, cg, at, z, bm, wg, wc, wx, *, off_m, tm, tn):
    m = og.shape[0]
    d = wg.shape[1]
    tm, tn = _tile(m, tm), _tile(d, tn)
    assert off_m % tn == 0
    nb = bm.shape[0]

    def res(a):
        return pl.BlockSpec((tm, a.shape[1]), lambda i, j: (i, 0))

    def wspec(a):
        return pl.BlockSpec((a.shape[0], tn), lambda i, j: (0, j))

    def mspec(k):
        return pl.BlockSpec((tm, tn), lambda i, j: (i, (off_m + k * d) // tn + j))

    return pl.pallas_call(
        _mix_kernel,
        grid=(m // tm, d // tn),
        in_specs=[res(og), res(cg), res(at), mspec(0), mspec(1), mspec(2),
                  pl.BlockSpec((nb, tn), lambda i, j: (0, j)),
                  wspec(wg), wspec(wc), wspec(wx)],
        out_specs=pl.BlockSpec((tm, tn), lambda i, j: (i, j)),
        out_shape=jax.ShapeDtypeStruct((m, d), BF16),
        compiler_params=_cparams("parallel", "arbitrary"),
        name="mix",
    )(og, cg, at, z, z, z, bm, wg, wc, wx)


def _outproj_kernel(a_ref, w_ref, x_ref, o_ref):
    o_ref[...] = x_ref[...] + _dot(a_ref[...], w_ref[...])


def _outproj(a, w, x, *, tm, tn):
    m, k = a.shape
    n = w.shape[1]
    tm, tn = _tile(m, tm), _tile(n, tn)
    return pl.pallas_call(
        _outproj_kernel,
        grid=(m // tm, n // tn),
        in_specs=[pl.BlockSpec((tm, k), lambda i, j: (i, 0)),
                  pl.BlockSpec((k, tn), lambda i, j: (0, j)),
                  pl.BlockSpec((tm, tn), lambda i, j: (i, j))],
        out_specs=pl.BlockSpec((tm, tn), lambda i, j: (i, j)),
        out_shape=jax.ShapeDtypeStruct((m, n), F32),
        compiler_params=_cparams("parallel", "arbitrary"),
        name="outproj",
    )(a, w, x)


def _ffn_kernel(x_ref, g_ref, wg_ref, wu_ref, wd_ref, gf_ref, o_ref, hn_ref, *, final_norm):
    f = pl.program_id(1)
    tm = x_ref.shape[0]

    @pl.when(f == 0)
    def _():
        g = g_ref[...]

        def init(sl):
            x = x_ref[sl, :]
            hn_ref[sl, :] = _rms(x, g).astype(BF16)
            o_ref[sl, :] = x

        _rows_loop(tm, 256, init)

    hn = hn_ref[...]
    gate = _dot(hn, wg_ref[...])
    up = _dot(hn, wu_ref[...])
    h = (gate * jax.nn.sigmoid(gate)) * up
    o_ref[...] += _dot(h.astype(BF16), wd_ref[...])

    if final_norm:
        @pl.when(f == pl.num_programs(1) - 1)
        def _():
            gf = gf_ref[...]

            def fin(sl):
                o_ref[sl, :] = _rms(o_ref[sl, :], gf)

            _rows_loop(tm, 256, fin)


def _ffn(x, g, wg, wu, wd, gf, *, final_norm, tm, tf):
    m, d = x.shape
    dff = wg.shape[1]
    tm, tf = _tile(m, tm), _tile(dff, tf)
    return pl.pallas_call(
        functools.partial(_ffn_kernel, final_norm=final_norm),
        grid=(m // tm, dff // tf),
        in_specs=[pl.BlockSpec((tm, d), lambda i, f: (i, 0), pipeline_mode=pl.Buffered(1)),
                  pl.BlockSpec((1, d), lambda i, f: (0, 0)),
                  pl.BlockSpec((d, tf), lambda i, f: (0, f)),
                  pl.BlockSpec((d, tf), lambda i, f: (0, f)),
                  pl.BlockSpec((tf, d), lambda i, f: (f, 0)),
                  pl.BlockSpec((1, d), lambda i, f: (0, 0))],
        out_specs=pl.BlockSpec((tm, d), lambda i, f: (i, 0)),
        out_shape=jax.ShapeDtypeStruct((m, d), F32),
        scratch_shapes=[pltpu.VMEM((tm, d), BF16)],
        compiler_params=_cparams("parallel", "arbitrary"),
        name="ffn",
    )(x, g, wg, wu, wd, gf)


def _layer(x, batch, seq, s0, buf, mem_k, mem_v, koff, voff, p, dims, *, final_norm):
    d, gk, gv, cw, xw, heads, dk, dv, xh, dh = dims
    off_q, off_k, off_v, off_r = 0, gk, 2 * gk, 2 * gk + gv
    off_cb = off_r + gv
    off_xq = off_cb + 3 * cw
    off_m = off_xq + xw
    z, a_lr = _inproj(x, p["g_mix"], p["w_main"], p["w_alr"], tm=1024, tn=512)
    og, s_new = _gla(z, a_lr, p["w_a2"], p["b_a"], p["g_gla_out"], s0, batch=batch, seq=seq, heads=heads,
                     dk=dk, dv=dv, off_q=off_q, off_k=off_k, off_v=off_v, off_r=off_r, tc=512, chunk=64)
    cg, new_buf = _conv(z, buf, p["w_conv"], batch=batch, seq=seq, cw=cw, off_cb=off_cb, tt=512)
    at = _xattn(z, mem_k, mem_v, batch=batch, seq=seq, heads=xh, dh=dh, off_q=off_xq,
                koff=koff, voff=voff, tt=512)
    mixed = _mix(og, cg, at, z, p["b_merge"], p["w_gla_o"], p["w_conv_o"], p["w_xa_o"],
                 off_m=off_m, tm=1024, tn=512)
    x1 = _outproj(mixed, p["w_out"], x, tm=1024, tn=512)
    x2 = _ffn(x1, p["g_ffn"], p["w_ffn_gate"], p["w_ffn_up"], p["w_ffn_down"], p["g_final"],
              final_norm=final_norm, tm=512, tf=256)
    return x2, s_new, new_buf


def kernel(x_prompt, x_sample, state_gla, cache_conv, cache_mem_k, cache_mem_v, mem_prompt, g_mix, w_in, w_a2, b_a, g_gla_out, w_gla_o, w_conv, w_conv_o, w_xa_o, g_mem, w_mem_kv, b_merge, w_out, g_ffn, w_ffn_gate, w_ffn_up, w_ffn_down, g_final):
    depth, bs, heads, dk, dv = state_gla.shape
    bp, tp, d = x_prompt.shape
    _, ts, _ = x_sample.shape
    gk, gv = heads * dk, heads * dv
    cw = cache_conv.shape[-1]
    _, _, n_mem, xh, dh = cache_mem_k.shape
    xw = xh * dh
    rank = w_a2.shape[1]
    dims = (d, gk, gv, cw, xw, heads, dk, dv, xh, dh)
    off_a = 2 * gk + gv
    assert rank <= LANES

    hp = x_prompt.reshape(bp * tp, d)
    hs = x_sample.reshape(bs * ts, d)
    row = lambda a: a.reshape(1, -1)
    outs = [[] for _ in range(6)]
    for l in range(depth):
        wl = w_in[l]
        p = {
            "g_mix": row(g_mix[l]),
            "w_main": jnp.concatenate([wl[:, :off_a], wl[:, off_a + rank:]], axis=1).astype(BF16),
            "w_alr": jnp.pad(wl[:, off_a:off_a + rank], ((0, 0), (0, LANES - rank))).astype(BF16),
            "w_a2": jnp.pad(w_a2[l], ((0, LANES - rank), (0, 0))).astype(BF16),
            "b_a": row(b_a[l]),
            "g_gla_out": row(g_gla_out[l]),
            "w_gla_o": w_gla_o[l].astype(BF16),
            "w_conv": w_conv[l],
            "w_conv_o": w_conv_o[l].astype(BF16),
            "w_xa_o": w_xa_o[l].astype(BF16),
            "b_merge": b_merge[l],
            "w_out": w_out[l].astype(BF16),
            "g_ffn": row(g_ffn[l]),
            "w_ffn_gate": w_ffn_gate[l].astype(BF16),
            "w_ffn_up": w_ffn_up[l].astype(BF16),
            "w_ffn_down": w_ffn_down[l].astype(BF16),
            "g_final": row(g_final),
        }
        last = l == depth - 1
        kv = _norm_matmul(mem_prompt.reshape(bp * n_mem, d), row(g_mem[l]), w_mem_kv[l].astype(BF16),
                          tm=512, tn=512).reshape(bp, n_mem, 2 * xw)
        s0 = jnp.zeros((bp, heads, dk, dv), state_gla.dtype)
        buf0 = jnp.zeros((bp,) + cache_conv.shape[2:], cache_conv.dtype)
        hp, sp, bufp = _layer(hp, bp, tp, s0, buf0, kv, kv, 0, xh, p, dims, final_norm=last)
        mk_s = cache_mem_k[l].reshape(bs, n_mem, xw)
        mv_s = cache_mem_v[l].reshape(bs, n_mem, xw)
        hs, ss, bufs = _layer(hs, bs, ts, state_gla[l], cache_conv[l], mk_s, mv_s, 0, 0, p, dims,
                              final_norm=last)
        new = (sp, bufp, kv[..., :xw].reshape(bp, n_mem, xh, dh), kv[..., xw:].reshape(bp, n_mem, xh, dh),
               ss, bufs)
        for acc, val in zip(outs, new):
            acc.append(val)
    return (hp.reshape(bp, tp, d), hs.reshape(bs, ts, d)) + tuple(jnp.stack(o) for o in outs)
```

```python
import functools

import jax
import jax.numpy as jnp
from jax import lax
from jax.experimental import pallas as pl
from jax.experimental.pallas import tpu as pltpu

F32 = jnp.float32
BF16 = jnp.bfloat16

EPS = 1e-6
GLA_BLOCK = 16
GATE_NORM = 16.0
LANES = 128
V7X_VMEM_BYTES = 64 * 1024 * 1024
VMEM_LIMIT = V7X_VMEM_BYTES - 4 * 1024 * 1024


def _cparams(*sem):
    return pltpu.CompilerParams(dimension_semantics=sem, vmem_limit_bytes=VMEM_LIMIT)


def _tile(n, pref):
    if n <= pref:
        return n
    t = pref
    while n % t:
        t //= 2
    return t


def _rms(x, g):
    ms = jnp.mean(x * x, axis=-1, keepdims=True)
    return x * lax.rsqrt(ms + EPS) * g


def _dot(a, b):
    return jnp.dot(a, b, preferred_element_type=F32)


def _dot_nt(a, b):
    return lax.dot_general(a, b, (((1,), (1,)), ((), ())), preferred_element_type=F32)


def _dot_tn(a, b):
    return lax.dot_general(a, b, (((0,), (0,)), ((), ())), preferred_element_type=F32)


ROW_BLOCK = 32


def _rows_loop(n_rows, fn):
    rb = min(ROW_BLOCK, n_rows)

    def body(i, c):
        fn(pl.ds(pl.multiple_of(i * rb, rb), rb))
        return c

    lax.fori_loop(0, n_rows // rb, body, 0)


def _drop_cols_kernel(a_ref, b_ref, o_ref, *, n_plain, shift):
    j = pl.program_id(1)
    tn = a_ref.shape[1]

    @pl.when(j < n_plain)
    def _():
        o_ref[...] = a_ref[...].astype(BF16)

    @pl.when(j >= n_plain)
    def _():
        ar = pltpu.roll(a_ref[...], tn - shift, 1)
        br = pltpu.roll(b_ref[...], LANES - shift, 1)
        lane = lax.broadcasted_iota(jnp.int32, br.shape, 1)
        o_ref[:, :tn - LANES] = ar[:, :tn - LANES].astype(BF16)
        o_ref[:, tn - LANES:] = jnp.where(lane < LANES - shift, ar[:, tn - LANES:], br).astype(BF16)


def _drop_cols(w, start, width, *, tr, tn):
    rows, cols = w.shape
    n_out = cols - width
    tr, tn = _tile(rows, tr), _tile(n_out, tn)
    assert start % tn == 0 and 0 < width < LANES and tn % LANES == 0 and tn > LANES
    lpt = tn // LANES
    return pl.pallas_call(
        functools.partial(_drop_cols_kernel, n_plain=start // tn, shift=width),
        grid=(rows // tr, n_out // tn),
        in_specs=[pl.BlockSpec((tr, tn), lambda i, j: (i, j)),
                  pl.BlockSpec((tr, LANES), lambda i, j: (i, (j + 1) * lpt))],
        out_specs=pl.BlockSpec((tr, tn), lambda i, j: (i, j)),
        out_shape=jax.ShapeDtypeStruct((rows, n_out), BF16),
        compiler_params=_cparams("parallel", "parallel"),
        name="wprep",
    )(w, w)


def _inproj_kernel(x_ref, g_ref, w_ref, wa_ref, z_ref, a_ref, xn_ref):
    @pl.when(pl.program_id(1) == 0)
    def _():
        g = g_ref[...]

        def norm(sl):
            xn_ref[sl, :] = _rms(x_ref[sl, :], g).astype(BF16)

        _rows_loop(x_ref.shape[0], norm)
        a_ref[...] = _dot(xn_ref[...], wa_ref[...])

    z_ref[...] = _dot(xn_ref[...], w_ref[...]).astype(z_ref.dtype)


def _inproj(x, g, w, wa, *, tm, tn):
    m, d = x.shape
    n = w.shape[1]
    tm, tn = _tile(m, tm), _tile(n, tn)
    return pl.pallas_call(
        _inproj_kernel,
        grid=(m // tm, n // tn),
        in_specs=[
            pl.BlockSpec((tm, d), lambda i, j: (i, 0), pipeline_mode=pl.Buffered(1)),
            pl.BlockSpec((1, d), lambda i, j: (0, 0)),
            pl.BlockSpec((d, tn), lambda i, j: (0, j)),
            pl.BlockSpec((d, LANES), lambda i, j: (0, 0)),
        ],
        out_specs=[
            pl.BlockSpec((tm, tn), lambda i, j: (i, j)),
            pl.BlockSpec((tm, LANES), lambda i, j: (i, 0)),
        ],
        out_shape=[jax.ShapeDtypeStruct((m, n), BF16), jax.ShapeDtypeStruct((m, LANES), F32)],
        scratch_shapes=[pltpu.VMEM((tm, d), BF16)],
        compiler_params=_cparams("parallel", "arbitrary"),
        name="inproj",
    )(x, g, w, wa)


def _norm_matmul_kernel(x_ref, g_ref, w_ref, o_ref, xn_ref):
    @pl.when(pl.program_id(1) == 0)
    def _():
        g = g_ref[...]

        def norm(sl):
            xn_ref[sl, :] = _rms(x_ref[sl, :], g).astype(BF16)

        _rows_loop(x_ref.shape[0], norm)

    o_ref[...] = _dot(xn_ref[...], w_ref[...])


def _norm_matmul(x, g, w, *, tm, tn):
    m, d = x.shape
    n = w.shape[1]
    tm, tn = _tile(m, tm), _tile(n, tn)
    return pl.pallas_call(
        _norm_matmul_kernel,
        grid=(m // tm, n // tn),
        in_specs=[
            pl.BlockSpec((tm, d), lambda i, j: (i, 0)),
            pl.BlockSpec((1, d), lambda i, j: (0, 0)),
            pl.BlockSpec((d, tn), lambda i, j: (0, j)),
        ],
        out_specs=pl.BlockSpec((tm, tn), lambda i, j: (i, j)),
        out_shape=jax.ShapeDtypeStruct((m, n), F32),
        scratch_shapes=[pltpu.VMEM((tm, d), BF16)],
        compiler_params=_cparams("parallel", "arbitrary"),
        name="memkv",
    )(x, g, w)


def _chunk_cumsum(x, pos, c):
    s = 1
    while s < c:
        x = x + jnp.where(pos >= s, pltpu.roll(x, s, 0), 0.0)
        s *= 2
    return x


def _group_end(x, h):
    n, w = x.shape
    return jnp.concatenate(
        [jnp.broadcast_to(x[(g + 1) * h - 1:(g + 1) * h, :], (h, w)) for g in range(n // h)], axis=0)


def _gla_kernel(q_ref, k_ref, v_ref, r_ref, a_ref, wa2_ref, ba_ref, g_ref, s0_ref,
                og_ref, sout_ref, s_ref, *, chunk, scale, dk, dv):
    t = pl.program_id(2)

    @pl.when(t == 0)
    def _():
        s_ref[...] = s0_ref[0]

    tc = q_ref.shape[0]
    hb = s_ref.shape[0]
    c = chunk
    nch = tc // c
    pos = lax.broadcasted_iota(jnp.int32, (tc, dk), 0) & (c - 1)
    ri = lax.broadcasted_iota(jnp.int32, (c, c), 0)
    ci = lax.broadcasted_iota(jnp.int32, (c, c), 1)
    hs = []
    h = GLA_BLOCK
    while h <= c:
        hs.append(h)
        h *= 2
    blk = GLA_BLOCK.bit_length() - 1
    mask0 = (lax.shift_right_logical(ri, blk) == lax.shift_right_logical(ci, blk)) & (ci <= ri)
    masks = {}
    for h in hs[:-1]:
        sh = (2 * h).bit_length() - 1
        same = lax.shift_right_logical(ri, sh) == lax.shift_right_logical(ci, sh)
        masks[h] = same & ((ri & (2 * h - 1)) >= h) & ((ci & (2 * h - 1)) < h)

    g = g_ref[...]
    a_bf = a_ref[...].astype(BF16)

    for hh in range(hb):
        kcol = slice(hh * dk, (hh + 1) * dk)
        vcol = slice(hh * dv, (hh + 1) * dv)
        la = jax.nn.log_sigmoid(_dot(a_bf, wa2_ref[:, kcol]) + ba_ref[:, kcol]) / GATE_NORM
        beta = _chunk_cumsum(la, pos, c)
        q = q_ref[:, kcol].astype(F32) * scale
        k = k_ref[:, kcol].astype(F32)
        qf, kf = {}, {}
        for h in hs:
            e_h = _group_end(beta, h)
            if h == c:
                s_h = jnp.zeros_like(beta)
                e_c = e_h
            else:
                s_h = jnp.where(pos >= h, pltpu.roll(e_h, h, 0), 0.0)
            qf[h] = (q * jnp.exp(beta - s_h)).astype(BF16)
            kf[h] = (k * jnp.exp(e_h - beta)).astype(BF16)
            if h == GLA_BLOCK:
                k_in = (k * jnp.exp(s_h - beta)).astype(BF16)
        dec_rows = jnp.exp(e_c)

        a_mats, incs, vs = [], [], []
        for i in range(nch):
            sl = slice(i * c, (i + 1) * c)
            a_mat = jnp.where(mask0, _dot_nt(qf[GLA_BLOCK][sl], k_in[sl]), 0.0)
            for h in hs[:-1]:
                a_mat = a_mat + jnp.where(masks[h], _dot_nt(qf[h][sl], kf[h][sl]), 0.0)
            v = v_ref[sl, vcol]
            a_mats.append(a_mat.astype(BF16))
            vs.append(v)
            incs.append(_dot_tn(kf[c][sl], v))

        s_val = s_ref[hh]
        starts = []
        for i in range(nch):
            starts.append(s_val.astype(BF16))
            dec = jnp.transpose(dec_rows[i * c:i * c + 8, :])[:, 0:1]
            s_val = s_val * dec + incs[i]
        s_ref[hh] = s_val

        for i in range(nch):
            sl = slice(i * c, (i + 1) * c)
            o = _dot(a_mats[i], vs[i]) + _dot(qf[c][sl], starts[i])
            on = _rms(o, g)
            r = r_ref[sl, vcol].astype(F32)
            og_ref[sl, vcol] = (on * (r * jax.nn.sigmoid(r))).astype(og_ref.dtype)

    @pl.when(t == pl.num_programs(2) - 1)
    def _():
        sout_ref[0] = s_ref[...]


def _gla(z, a_lr, wa2, ba, g, s0, *, batch, seq, heads, dk, dv, off_q, off_k, off_v, off_r, tc, chunk, hb):
    m = z.shape[0]
    tc = _tile(seq, tc)
    chunk = min(chunk, tc)
    nt = seq // tc
    hb = min(hb, heads)
    wk, wv = hb * dk, hb * dv
    assert heads % hb == 0
    assert off_q % wk == 0 and off_k % wk == 0 and off_v % wv == 0 and off_r % wv == 0
    assert chunk % GLA_BLOCK == 0 and tc % chunk == 0

    def zspec(width, off):
        return pl.BlockSpec((tc, width), lambda b, h, t: (b * nt + t, off // width + h))

    return pl.pallas_call(
        functools.partial(_gla_kernel, chunk=chunk, scale=float(dk) ** -0.5, dk=dk, dv=dv),
        grid=(batch, heads // hb, nt),
        in_specs=[
            zspec(wk, off_q), zspec(wk, off_k), zspec(wv, off_v), zspec(wv, off_r),
            pl.BlockSpec((tc, LANES), lambda b, h, t: (b * nt + t, 0)),
            pl.BlockSpec((LANES, wk), lambda b, h, t: (0, h)),
            pl.BlockSpec((1, wk), lambda b, h, t: (0, h)),
            pl.BlockSpec((1, dv), lambda b, h, t: (0, 0)),
            pl.BlockSpec((1, hb, dk, dv), lambda b, h, t: (b, h, 0, 0)),
        ],
        out_specs=[
            pl.BlockSpec((tc, wv), lambda b, h, t: (b * nt + t, h)),
            pl.BlockSpec((1, hb, dk, dv), lambda b, h, t: (b, h, 0, 0)),
        ],
        out_shape=[jax.ShapeDtypeStruct((m, heads * dv), BF16),
                   jax.ShapeDtypeStruct((batch, heads, dk, dv), F32)],
        scratch_shapes=[pltpu.VMEM((hb, dk, dv), F32)],
        compiler_params=_cparams("parallel", "parallel", "arbitrary"),
        name="gla",
    )(z, z, z, z, a_lr, wa2, ba, g, s0)


def _conv_kernel(cb_ref, cc_ref, ch_ref, buf_ref, w_ref, cg_ref, nb_ref, tail_ref):
    t = pl.program_id(1)

    @pl.when(t == 0)
    def _():
        tail_ref[0:2, :] = buf_ref[0]

    tt = cc_ref.shape[0]
    u = cc_ref[...].astype(F32) * ch_ref[...].astype(F32)
    rows = lax.broadcasted_iota(jnp.int32, u.shape, 0)
    p0 = tail_ref[0:1, :]
    p1 = tail_ref[1:2, :]
    u1 = jnp.where(rows == 0, p1, pltpu.roll(u, 1, 0))
    u2 = jnp.where(rows == 0, p0, jnp.where(rows == 1, p1, pltpu.roll(u, 2, 0)))
    w = w_ref[...]
    conv = u2 * w[0:1, :] + u1 * w[1:2, :] + u * w[2:3, :]
    cg_ref[...] = (cb_ref[...].astype(F32) * conv).astype(cg_ref.dtype)
    last2 = u[tt - 2:tt, :]
    tail_ref[0:2, :] = last2

    @pl.when(t == pl.num_programs(1) - 1)
    def _():
        nb_ref[0] = last2


def _conv(z, buf, w, *, batch, seq, cw, off_cb, tt):
    m = z.shape[0]
    tt = _tile(seq, tt)
    nt = seq // tt
    assert off_cb % cw == 0 and tt >= 2
    kk = buf.shape[1]

    def zspec(k):
        return pl.BlockSpec((tt, cw), lambda b, t: (b * nt + t, off_cb // cw + k))

    return pl.pallas_call(
        _conv_kernel,
        grid=(batch, nt),
        in_specs=[zspec(0), zspec(1), zspec(2),
                  pl.BlockSpec((1, kk, cw), lambda b, t: (b, 0, 0)),
                  pl.BlockSpec((kk + 1, cw), lambda b, t: (0, 0))],
        out_specs=[pl.BlockSpec((tt, cw), lambda b, t: (b * nt + t, 0)),
                   pl.BlockSpec((1, kk, cw), lambda b, t: (b, 0, 0))],
        out_shape=[jax.ShapeDtypeStruct((m, cw), BF16), jax.ShapeDtypeStruct((batch, kk, cw), F32)],
        scratch_shapes=[pltpu.VMEM((8, cw), F32)],
        compiler_params=_cparams("parallel", "arbitrary"),
        name="conv",
    )(z, z, z, buf, w)


def _xattn_kernel(q_ref, *refs, scale, heads, dh):
    if len(refs) == 3:
        ks = pltpu.einshape("mhd->hmd", refs[0][...].astype(BF16))
        vs = pltpu.einshape("mhd->hmd", refs[1][...].astype(BF16))
        k_of, v_of = (lambda h: ks[h]), (lambda h: vs[h])
    else:
        k_of = lambda h: refs[h][...].astype(BF16)
        v_of = lambda h: refs[heads + h][...].astype(BF16)
    o_ref = refs[-1]
    for h in range(heads):
        col = slice(h * dh, (h + 1) * dh)
        s = _dot_nt(q_ref[:, col], k_of(h)) * scale
        p = jnp.exp(s - jnp.max(s, axis=-1, keepdims=True))
        p = p / jnp.sum(p, axis=-1, keepdims=True)
        o_ref[:, col] = _dot(p.astype(BF16), v_of(h)).astype(o_ref.dtype)


def _xattn(z, mem_k, mem_v, *, batch, seq, heads, dh, off_q, voff, tt):
    m = z.shape[0]
    tt = _tile(seq, tt)
    nt = seq // tt
    nm = mem_k.shape[1]
    xw = heads * dh
    assert off_q % xw == 0
    if mem_k.ndim == 4:
        mem_specs = [pl.BlockSpec((None, nm, heads, dh), lambda b, t: (b, 0, 0, 0))] * 2
        mems = (mem_k, mem_v)
    else:
        def spec(h, off):
            return pl.BlockSpec((None, nm, dh), lambda b, t: (b, 0, off * heads + h))
        mem_specs = [spec(h, 0) for h in range(heads)] + [spec(h, voff) for h in range(heads)]
        mems = (mem_k,) * heads + (mem_v,) * heads
    return pl.pallas_call(
        functools.partial(_xattn_kernel, scale=float(dh) ** -0.5, heads=heads, dh=dh),
        grid=(batch, nt),
        in_specs=[pl.BlockSpec((tt, xw), lambda b, t: (b * nt + t, off_q // xw))] + mem_specs,
        out_specs=pl.BlockSpec((tt, xw), lambda b, t: (b * nt + t, 0)),
        out_shape=jax.ShapeDtypeStruct((m, xw), BF16),
        compiler_params=_cparams("parallel", "parallel"),
        name="xattn",
    )(z, *mems)


def _mix_kernel(og_ref, cg_ref, at_ref, m0_ref, m1_ref, m2_ref, bm_ref, wg_ref, wc_ref, wx_ref, o_ref):
    bm = bm_ref[...]
    yg = _dot(og_ref[...], wg_ref[...])
    yc = _dot(cg_ref[...], wc_ref[...])
    yx = _dot(at_ref[...], wx_ref[...])
    mixed = (jax.nn.sigmoid(m0_ref[...].astype(F32) + bm[0:1, :]) * yg
             + jax.nn.sigmoid(m1_ref[...].astype(F32) + bm[1:2, :]) * yc
             + jax.nn.sigmoid(m2_ref[...].astype(F32) + bm[2:3, :]) * yx)
    o_ref[...] = mixed.astype(o_ref.dtype)


def _mix(og, cg, at, z, bm, wg, wc, wx, *, off_m, tm, tn):
    m = og.shape[0]
    d = wg.shape[1]
    tm, tn = _tile(m, tm), _tile(d, tn)
    assert off_m % tn == 0
    nb = bm.shape[0]

    def res(a):
        return pl.BlockSpec((tm, a.shape[1]), lambda i, j: (i, 0))

    def wspec(a):
        return pl.BlockSpec((a.shape[0], tn), lambda i, j: (0, j))

    def mspec(k):
        return pl.BlockSpec((tm, tn), lambda i, j: (i, (off_m + k * d) // tn + j))

    return pl.pallas_call(
        _mix_kernel,
        grid=(m // tm, d // tn),
        in_specs=[res(og), res(cg), res(at), mspec(0), mspec(1), mspec(2),
                  pl.BlockSpec((nb, tn), lambda i, j: (0, j)),
                  wspec(wg), wspec(wc), wspec(wx)],
        out_specs=pl.BlockSpec((tm, tn), lambda i, j: (i, j)),
        out_shape=jax.ShapeDtypeStruct((m, d), BF16),
        compiler_params=_cparams("parallel", "arbitrary"),
        name="mix",
    )(og, cg, at, z, z, z, bm, wg, wc, wx)


def _outproj_kernel(a_ref, w_ref, x_ref, o_ref):
    o_ref[...] = x_ref[...] + _dot(a_ref[...], w_ref[...])


def _outproj(a, w, x, *, tm, tn):
    m, k = a.shape
    n = w.shape[1]
    tm, tn = _tile(m, tm), _tile(n, tn)
    return pl.pallas_call(
        _outproj_kernel,
        grid=(m // tm, n // tn),
        in_specs=[pl.BlockSpec((tm, k), lambda i, j: (i, 0)),
                  pl.BlockSpec((k, tn), lambda i, j: (0, j)),
                  pl.BlockSpec((tm, tn), lambda i, j: (i, j))],
        out_specs=pl.BlockSpec((tm, tn), lambda i, j: (i, j)),
        out_shape=jax.ShapeDtypeStruct((m, n), F32),
        compiler_params=_cparams("parallel", "arbitrary"),
        name="outproj",
    )(a, w, x)


def _ffn_kernel(x_ref, g_ref, wg_ref, wu_ref, wd_ref, gf_ref, o_ref, hn_ref, *, final_norm):
    f = pl.program_id(1)
    tm = x_ref.shape[0]

    @pl.when(f == 0)
    def _():
        g = g_ref[...]

        def init(sl):
            x = x_ref[sl, :]
            hn_ref[sl, :] = _rms(x, g).astype(BF16)
            o_ref[sl, :] = x

        _rows_loop(tm, init)

    hn = hn_ref[...]
    gate = _dot(hn, wg_ref[...])
    up = _dot(hn, wu_ref[...])
    h = ((gate * jax.nn.sigmoid(gate)) * up).astype(BF16)
    nb = min(FFN_NB, o_ref.shape[1])
    for n in range(o_ref.shape[1] // nb):
        cols = slice(n * nb, (n + 1) * nb)
        o_ref[:, cols] += _dot(h, wd_ref[:, cols])

    if final_norm:
        @pl.when(f == pl.num_programs(1) - 1)
        def _():
            gf = gf_ref[...]

            def fin(sl):
                o_ref[sl, :] = _rms(o_ref[sl, :], gf)

            _rows_loop(tm, fin)


def _ffn(x, g, wg, wu, wd, gf, *, final_norm, tm, tf):
    m, d = x.shape
    dff = wg.shape[1]
    tm, tf = _tile(m, tm), _tile(dff, tf)
    return pl.pallas_call(
        functools.partial(_ffn_kernel, final_norm=final_norm),
        grid=(m // tm, dff // tf),
        in_specs=[pl.BlockSpec((tm, d), lambda i, f: (i, 0), pipeline_mode=pl.Buffered(1)),
                  pl.BlockSpec((1, d), lambda i, f: (0, 0)),
                  pl.BlockSpec((d, tf), lambda i, f: (0, f)),
                  pl.BlockSpec((d, tf), lambda i, f: (0, f)),
                  pl.BlockSpec((tf, d), lambda i, f: (f, 0)),
                  pl.BlockSpec((1, d), lambda i, f: (0, 0))],
        out_specs=pl.BlockSpec((tm, d), lambda i, f: (i, 0)),
        out_shape=jax.ShapeDtypeStruct((m, d), F32),
        scratch_shapes=[pltpu.VMEM((tm, d), BF16)],
        compiler_params=_cparams("parallel", "arbitrary"),
        name="ffn",
    )(x, g, wg, wu, wd, gf)


GLA_CHUNK = 128
GLA_BODIES = 4
FFN_TF = 512
FFN_NB = 1024


def _layer(x, batch, seq, s0, buf, mem_k, mem_v, voff, p, dims, *, final_norm):
    d, gk, gv, cw, xw, heads, dk, dv, xh, dh = dims
    off_q, off_k, off_v, off_r = 0, gk, 2 * gk, 2 * gk + gv
    off_cb = off_r + gv
    off_xq = off_cb + 3 * cw
    off_m = off_xq + xw
    z, a_lr = _inproj(x, p["g_mix"], p["w_main"], p["w_alr"], tm=1024, tn=1024)
    tc = _tile(seq, GLA_CHUNK * GLA_BODIES)
    hb = max(1, GLA_BODIES // max(1, tc // GLA_CHUNK))
    og, s_new = _gla(z, a_lr, p["w_a2"], p["b_a"], p["g_gla_out"], s0, batch=batch, seq=seq, heads=heads,
                     dk=dk, dv=dv, off_q=off_q, off_k=off_k, off_v=off_v, off_r=off_r,
                     tc=tc, chunk=GLA_CHUNK, hb=hb)
    cg, new_buf = _conv(z, buf, p["w_conv"], batch=batch, seq=seq, cw=cw, off_cb=off_cb, tt=512)
    at = _xattn(z, mem_k, mem_v, batch=batch, seq=seq, heads=xh, dh=dh, off_q=off_xq, voff=voff, tt=1024)
    mixed = _mix(og, cg, at, z, p["b_merge"], p["w_gla_o"], p["w_conv_o"], p["w_xa_o"],
                 off_m=off_m, tm=1024, tn=1024)
    x1 = _outproj(mixed, p["w_out"], x, tm=1024, tn=1024)
    x2 = _ffn(x1, p["g_ffn"], p["w_ffn_gate"], p["w_ffn_up"], p["w_ffn_down"], p["g_final"],
              final_norm=final_norm, tm=512, tf=FFN_TF)
    return x2, s_new, new_buf


def kernel(x_prompt, x_sample, state_gla, cache_conv, cache_mem_k, cache_mem_v, mem_prompt, g_mix, w_in, w_a2, b_a, g_gla_out, w_gla_o, w_conv, w_conv_o, w_xa_o, g_mem, w_mem_kv, b_merge, w_out, g_ffn, w_ffn_gate, w_ffn_up, w_ffn_down, g_final):
    depth, bs, heads, dk, dv = state_gla.shape
    bp, tp, d = x_prompt.shape
    _, ts, _ = x_sample.shape
    gk, gv = heads * dk, heads * dv
    cw = cache_conv.shape[-1]
    _, _, n_mem, xh, dh = cache_mem_k.shape
    xw = xh * dh
    rank = w_a2.shape[1]
    dims = (d, gk, gv, cw, xw, heads, dk, dv, xh, dh)
    off_a = 2 * gk + gv
    assert rank <= LANES
    ff_pad = -w_ffn_gate.shape[-1] % FFN_TF

    hp = x_prompt.reshape(bp * tp, d)
    hs = x_sample.reshape(bs * ts, d)
    row = lambda a: a.reshape(1, -1)
    outs = [[] for _ in range(6)]
    for l in range(depth):
        wl = w_in[l]
        p = {
            "g_mix": row(g_mix[l]),
            "w_main": _drop_cols(wl, off_a, rank, tr=512, tn=1024),
            "w_alr": jnp.pad(wl[:, off_a:off_a + rank], ((0, 0), (0, LANES - rank))).astype(BF16),
            "w_a2": jnp.pad(w_a2[l], ((0, LANES - rank), (0, 0))).astype(BF16),
            "b_a": row(b_a[l]),
            "g_gla_out": row(g_gla_out[l]),
            "w_gla_o": w_gla_o[l].astype(BF16),
            "w_conv": w_conv[l],
            "w_conv_o": w_conv_o[l].astype(BF16),
            "w_xa_o": w_xa_o[l].astype(BF16),
            "b_merge": b_merge[l],
            "w_out": w_out[l].astype(BF16),
            "g_ffn": row(g_ffn[l]),
            "w_ffn_gate": jnp.pad(w_ffn_gate[l], ((0, 0), (0, ff_pad))).astype(BF16),
            "w_ffn_up": jnp.pad(w_ffn_up[l], ((0, 0), (0, ff_pad))).astype(BF16),
            "w_ffn_down": jnp.pad(w_ffn_down[l], ((0, ff_pad), (0, 0))).astype(BF16),
            "g_final": row(g_final),
        }
        last = l == depth - 1
        kv = _norm_matmul(mem_prompt.reshape(bp * n_mem, d), row(g_mem[l]), w_mem_kv[l].astype(BF16),
                          tm=512, tn=512).reshape(bp, n_mem, 2 * xw)
        s0 = jnp.zeros((bp, heads, dk, dv), state_gla.dtype)
        buf0 = jnp.zeros((bp,) + cache_conv.shape[2:], cache_conv.dtype)
        hp, sp, bufp = _layer(hp, bp, tp, s0, buf0, kv, kv, 1, p, dims, final_norm=last)
        hs, ss, bufs = _layer(hs, bs, ts, state_gla[l], cache_conv[l], cache_mem_k[l], cache_mem_v[l], 0,
                              p, dims, final_norm=last)
        new = (sp, bufp, kv[..., :xw].reshape(bp, n_mem, xh, dh), kv[..., xw:].reshape(bp, n_mem, xh, dh),
               ss, bufs)
        for acc, val in zip(outs, new):
            acc.append(val)
    return (hp.reshape(bp, tp, d), hs.reshape(bs, ts, d)) + tuple(jnp.stack(o) for o in outs)
```

```python
import functools

import jax
import jax.numpy as jnp
from jax import lax
from jax.experimental import pallas as pl
from jax.experimental.pallas import tpu as pltpu

F32 = jnp.float32
BF16 = jnp.bfloat16

EPS = 1e-6
GLA_BLOCK = 16
GATE_NORM = 16.0
LANES = 128
V7X_VMEM_BYTES = 64 * 1024 * 1024
VMEM_LIMIT = V7X_VMEM_BYTES - 4 * 1024 * 1024


def _cparams(*sem):
    return pltpu.CompilerParams(dimension_semantics=sem, vmem_limit_bytes=VMEM_LIMIT)


def _tile(n, pref):
    if n <= pref:
        return n
    t = pref
    while n % t:
        t //= 2
    return t


def _rms(x, g):
    ms = jnp.mean(x * x, axis=-1, keepdims=True)
    return x * lax.rsqrt(ms + EPS) * g


def _dot(a, b):
    return jnp.dot(a, b, preferred_element_type=F32)


def _dot_nt(a, b):
    return lax.dot_general(a, b, (((1,), (1,)), ((), ())), preferred_element_type=F32)


def _dot_tn(a, b):
    return lax.dot_general(a, b, (((0,), (0,)), ((), ())), preferred_element_type=F32)


ROW_BLOCK = 32


def _rows_loop(n_rows, fn):
    rb = min(ROW_BLOCK, n_rows)

    def body(i, c):
        fn(pl.ds(pl.multiple_of(i * rb, rb), rb))
        return c

    lax.fori_loop(0, n_rows // rb, body, 0)


def _drop_rows_t_kernel(a_ref, b_ref, o_ref, *, n_plain, shift):
    j = pl.program_id(0)

    @pl.when(j < n_plain)
    def _():
        o_ref[...] = a_ref[...].T.astype(BF16)

    @pl.when(j >= n_plain)
    def _():
        rows = jnp.concatenate([a_ref[shift:, :], b_ref[...]], axis=0)
        o_ref[...] = rows.T.astype(BF16)


def _drop_rows_t(wt, start, width, *, tn, tk):
    rows, cols = wt.shape
    n_out = rows - width
    tn, tk = _tile(n_out, tn), _tile(cols, tk)
    assert start % tn == 0 and width % 8 == 0 and tn % width == 0
    return pl.pallas_call(
        functools.partial(_drop_rows_t_kernel, n_plain=start // tn, shift=width),
        grid=(n_out // tn, cols // tk),
        in_specs=[pl.BlockSpec((tn, tk), lambda j, k: (j, k)),
                  pl.BlockSpec((width, tk), lambda j, k: ((j + 1) * (tn // width), k))],
        out_specs=pl.BlockSpec((tk, tn), lambda j, k: (k, j)),
        out_shape=jax.ShapeDtypeStruct((cols, n_out), BF16),
        compiler_params=_cparams("parallel", "parallel"),
        name="wprep",
    )(wt, wt)


def _rows_t_pad_kernel(a_ref, o_ref):
    a = a_ref[...]
    full = jnp.concatenate([a, jnp.zeros((LANES - a.shape[0], a.shape[1]), a.dtype)], axis=0)
    o_ref[...] = full.T.astype(BF16)


def _rows_t_pad(wt, start, width, *, tk):
    cols = wt.shape[1]
    tk = _tile(cols, tk)
    assert start % width == 0 and width % 8 == 0 and width <= LANES
    return pl.pallas_call(
        _rows_t_pad_kernel,
        grid=(cols // tk,),
        in_specs=[pl.BlockSpec((width, tk), lambda k: (start // width, k))],
        out_specs=pl.BlockSpec((tk, LANES), lambda k: (k, 0)),
        out_shape=jax.ShapeDtypeStruct((cols, LANES), BF16),
        compiler_params=_cparams("parallel"),
        name="wprep_alr",
    )(wt)


def _inproj_kernel(x_ref, g_ref, w_ref, wa_ref, z_ref, a_ref, xn_ref):
    @pl.when(pl.program_id(1) == 0)
    def _():
        g = g_ref[...]

        def norm(sl):
            xn_ref[sl, :] = _rms(x_ref[sl, :], g).astype(BF16)

        _rows_loop(x_ref.shape[0], norm)
        a_ref[...] = _dot(xn_ref[...], wa_ref[...])

    z_ref[...] = _dot(xn_ref[...], w_ref[...]).astype(z_ref.dtype)


def _inproj(x, g, w, wa, *, tm, tn):
    m, d = x.shape
    n = w.shape[1]
    tm, tn = _tile(m, tm), _tile(n, tn)
    return pl.pallas_call(
        _inproj_kernel,
        grid=(m // tm, n // tn),
        in_specs=[
            pl.BlockSpec((tm, d), lambda i, j: (i, 0), pipeline_mode=pl.Buffered(1)),
            pl.BlockSpec((1, d), lambda i, j: (0, 0)),
            pl.BlockSpec((d, tn), lambda i, j: (0, j)),
            pl.BlockSpec((d, LANES), lambda i, j: (0, 0)),
        ],
        out_specs=[
            pl.BlockSpec((tm, tn), lambda i, j: (i, j)),
            pl.BlockSpec((tm, LANES), lambda i, j: (i, 0)),
        ],
        out_shape=[jax.ShapeDtypeStruct((m, n), BF16), jax.ShapeDtypeStruct((m, LANES), F32)],
        scratch_shapes=[pltpu.VMEM((tm, d), BF16)],
        compiler_params=_cparams("parallel", "arbitrary"),
        name="inproj",
    )(x, g, w, wa)


def _norm_matmul_kernel(x_ref, g_ref, w_ref, o_ref, xn_ref):
    @pl.when(pl.program_id(1) == 0)
    def _():
        g = g_ref[...]

        def norm(sl):
            xn_ref[sl, :] = _rms(x_ref[sl, :], g).astype(BF16)

        _rows_loop(x_ref.shape[0], norm)

    o_ref[...] = _dot(xn_ref[...], w_ref[...])


def _norm_matmul(x, g, w, *, tm, tn):
    m, d = x.shape
    n = w.shape[1]
    tm, tn = _tile(m, tm), _tile(n, tn)
    return pl.pallas_call(
        _norm_matmul_kernel,
        grid=(m // tm, n // tn),
        in_specs=[
            pl.BlockSpec((tm, d), lambda i, j: (i, 0)),
            pl.BlockSpec((1, d), lambda i, j: (0, 0)),
            pl.BlockSpec((d, tn), lambda i, j: (0, j)),
        ],
        out_specs=pl.BlockSpec((tm, tn), lambda i, j: (i, j)),
        out_shape=jax.ShapeDtypeStruct((m, n), F32),
        scratch_shapes=[pltpu.VMEM((tm, d), BF16)],
        compiler_params=_cparams("parallel", "arbitrary"),
        name="memkv",
    )(x, g, w)


def _chunk_cumsum(x, pos, c):
    s = 1
    while s < c:
        x = x + jnp.where(pos >= s, pltpu.roll(x, s, 0), 0.0)
        s *= 2
    return x


def _group_end(x, h):
    n, w = x.shape
    return jnp.concatenate(
        [jnp.broadcast_to(x[(g + 1) * h - 1:(g + 1) * h, :], (h, w)) for g in range(n // h)], axis=0)


def _gla_kernel(q_ref, k_ref, v_ref, r_ref, a_ref, wa2_ref, ba_ref, g_ref, s0_ref,
                og_ref, sout_ref, s_ref, *, chunk, scale, dk, dv):
    t = pl.program_id(2)

    @pl.when(t == 0)
    def _():
        s_ref[...] = s0_ref[0]

    tc = q_ref.shape[0]
    hb = s_ref.shape[0]
    c = chunk
    nch = tc // c
    pos = lax.broadcasted_iota(jnp.int32, (tc, dk), 0) & (c - 1)
    ri = lax.broadcasted_iota(jnp.int32, (c, c), 0)
    ci = lax.broadcasted_iota(jnp.int32, (c, c), 1)
    hs = []
    h = GLA_BLOCK
    while h <= c:
        hs.append(h)
        h *= 2
    blk = GLA_BLOCK.bit_length() - 1
    mask0 = (lax.shift_right_logical(ri, blk) == lax.shift_right_logical(ci, blk)) & (ci <= ri)
    masks = {}
    for h in hs[:-1]:
        sh = (2 * h).bit_length() - 1
        same = lax.shift_right_logical(ri, sh) == lax.shift_right_logical(ci, sh)
        masks[h] = same & ((ri & (2 * h - 1)) >= h) & ((ci & (2 * h - 1)) < h)

    g = g_ref[...]
    a_bf = a_ref[...].astype(BF16)

    for hh in range(hb):
        kcol = slice(hh * dk, (hh + 1) * dk)
        vcol = slice(hh * dv, (hh + 1) * dv)
        la = jax.nn.log_sigmoid(_dot(a_bf, wa2_ref[:, kcol]) + ba_ref[:, kcol]) / GATE_NORM
        beta = _chunk_cumsum(la, pos, c)
        q = q_ref[:, kcol].astype(F32) * scale
        k = k_ref[:, kcol].astype(F32)
        qf, kf = {}, {}
        for h in hs:
            e_h = _group_end(beta, h)
            if h == c:
                s_h = jnp.zeros_like(beta)
                e_c = e_h
            else:
                s_h = jnp.where(pos >= h, pltpu.roll(e_h, h, 0), 0.0)
            qf[h] = (q * jnp.exp(beta - s_h)).astype(BF16)
            kf[h] = (k * jnp.exp(e_h - beta)).astype(BF16)
            if h == GLA_BLOCK:
                k_in = (k * jnp.exp(s_h - beta)).astype(BF16)
        dec_rows = jnp.exp(e_c)

        a_mats, incs, vs = [], [], []
        for i in range(nch):
            sl = slice(i * c, (i + 1) * c)
            a_mat = jnp.where(mask0, _dot_nt(qf[GLA_BLOCK][sl], k_in[sl]), 0.0)
            for h in hs[:-1]:
                a_mat = a_mat + jnp.where(masks[h], _dot_nt(qf[h][sl], kf[h][sl]), 0.0)
            v = v_ref[sl, vcol]
            a_mats.append(a_mat.astype(BF16))
            vs.append(v)
            incs.append(_dot_tn(kf[c][sl], v))

        s_val = s_ref[hh]
        starts = []
        for i in range(nch):
            starts.append(s_val.astype(BF16))
            dec = jnp.transpose(dec_rows[i * c:i * c + 8, :])[:, 0:1]
            s_val = s_val * dec + incs[i]
        s_ref[hh] = s_val

        for i in range(nch):
            sl = slice(i * c, (i + 1) * c)
            o = _dot(a_mats[i], vs[i]) + _dot(qf[c][sl], starts[i])
            on = _rms(o, g)
            r = r_ref[sl, vcol].astype(F32)
            og_ref[sl, vcol] = (on * (r * jax.nn.sigmoid(r))).astype(og_ref.dtype)

    @pl.when(t == pl.num_programs(2) - 1)
    def _():
        sout_ref[0] = s_ref[...]


def _gla(z, a_lr, wa2, ba, g, s0, *, batch, seq, heads, dk, dv, off_q, off_k, off_v, off_r, tc, chunk, hb):
    m = z.shape[0]
    tc = _tile(seq, tc)
    chunk = min(chunk, tc)
    nt = seq // tc
    hb = min(hb, heads)
    wk, wv = hb * dk, hb * dv
    assert heads % hb == 0
    assert off_q % wk == 0 and off_k % wk == 0 and off_v % wv == 0 and off_r % wv == 0
    assert chunk % GLA_BLOCK == 0 and tc % chunk == 0

    def zspec(width, off):
        return pl.BlockSpec((tc, width), lambda b, h, t: (b * nt + t, off // width + h))

    return pl.pallas_call(
        functools.partial(_gla_kernel, chunk=chunk, scale=float(dk) ** -0.5, dk=dk, dv=dv),
        grid=(batch, heads // hb, nt),
        in_specs=[
            zspec(wk, off_q), zspec(wk, off_k), zspec(wv, off_v), zspec(wv, off_r),
            pl.BlockSpec((tc, LANES), lambda b, h, t: (b * nt + t, 0)),
            pl.BlockSpec((LANES, wk), lambda b, h, t: (0, h)),
            pl.BlockSpec((1, wk), lambda b, h, t: (0, h)),
            pl.BlockSpec((1, dv), lambda b, h, t: (0, 0)),
            pl.BlockSpec((1, hb, dk, dv), lambda b, h, t: (b, h, 0, 0)),
        ],
        out_specs=[
            pl.BlockSpec((tc, wv), lambda b, h, t: (b * nt + t, h)),
            pl.BlockSpec((1, hb, dk, dv), lambda b, h, t: (b, h, 0, 0)),
        ],
        out_shape=[jax.ShapeDtypeStruct((m, heads * dv), BF16),
                   jax.ShapeDtypeStruct((batch, heads, dk, dv), F32)],
        scratch_shapes=[pltpu.VMEM((hb, dk, dv), F32)],
        compiler_params=_cparams("parallel", "parallel", "arbitrary"),
        name="gla",
    )(z, z, z, z, a_lr, wa2, ba, g, s0)


def _conv_kernel(cb_ref, cc_ref, ch_ref, buf_ref, w_ref, cg_ref, nb_ref, tail_ref):
    t = pl.program_id(1)

    @pl.when(t == 0)
    def _():
        tail_ref[0:2, :] = buf_ref[0]

    tt = cc_ref.shape[0]
    u = cc_ref[...].astype(F32) * ch_ref[...].astype(F32)
    rows = lax.broadcasted_iota(jnp.int32, u.shape, 0)
    p0 = tail_ref[0:1, :]
    p1 = tail_ref[1:2, :]
    u1 = jnp.where(rows == 0, p1, pltpu.roll(u, 1, 0))
    u2 = jnp.where(rows == 0, p0, jnp.where(rows == 1, p1, pltpu.roll(u, 2, 0)))
    w = w_ref[...]
    conv = u2 * w[0:1, :] + u1 * w[1:2, :] + u * w[2:3, :]
    cg_ref[...] = (cb_ref[...].astype(F32) * conv).astype(cg_ref.dtype)
    last2 = u[tt - 2:tt, :]
    tail_ref[0:2, :] = last2

    @pl.when(t == pl.num_programs(1) - 1)
    def _():
        nb_ref[0] = last2


def _conv(z, buf, w, *, batch, seq, cw, off_cb, tt):
    m = z.shape[0]
    tt = _tile(seq, tt)
    nt = seq // tt
    assert off_cb % cw == 0 and tt >= 2
    kk = buf.shape[1]

    def zspec(k):
        return pl.BlockSpec((tt, cw), lambda b, t: (b * nt + t, off_cb // cw + k))

    return pl.pallas_call(
        _conv_kernel,
        grid=(batch, nt),
        in_specs=[zspec(0), zspec(1), zspec(2),
                  pl.BlockSpec((1, kk, cw), lambda b, t: (b, 0, 0)),
                  pl.BlockSpec((kk + 1, cw), lambda b, t: (0, 0))],
        out_specs=[pl.BlockSpec((tt, cw), lambda b, t: (b * nt + t, 0)),
                   pl.BlockSpec((1, kk, cw), lambda b, t: (b, 0, 0))],
        out_shape=[jax.ShapeDtypeStruct((m, cw), BF16), jax.ShapeDtypeStruct((batch, kk, cw), F32)],
        scratch_shapes=[pltpu.VMEM((8, cw), F32)],
        compiler_params=_cparams("parallel", "arbitrary"),
        name="conv",
    )(z, z, z, buf, w)


def _xattn_kernel(q_ref, *refs, scale, heads, dh):
    if len(refs) == 3:
        ks = pltpu.einshape("mhd->hmd", refs[0][...].astype(BF16))
        vs = pltpu.einshape("mhd->hmd", refs[1][...].astype(BF16))
        k_of, v_of = (lambda h: ks[h]), (lambda h: vs[h])
    else:
        k_of = lambda h: refs[h][...].astype(BF16)
        v_of = lambda h: refs[heads + h][...].astype(BF16)
    o_ref = refs[-1]
    for h in range(heads):
        col = slice(h * dh, (h + 1) * dh)
        s = _dot_nt(q_ref[:, col], k_of(h)) * scale
        p = jnp.exp(s - jnp.max(s, axis=-1, keepdims=True))
        p = p / jnp.sum(p, axis=-1, keepdims=True)
        o_ref[:, col] = _dot(p.astype(BF16), v_of(h)).astype(o_ref.dtype)


def _xattn(z, mem_k, mem_v, *, batch, seq, heads, dh, off_q, voff, tt):
    m = z.shape[0]
    tt = _tile(seq, tt)
    nt = seq // tt
    nm = mem_k.shape[1]
    xw = heads * dh
    assert off_q % xw == 0
    if mem_k.ndim == 4:
        mem_specs = [pl.BlockSpec((None, nm, heads, dh), lambda b, t: (b, 0, 0, 0))] * 2
        mems = (mem_k, mem_v)
    else:
        def spec(h, off):
            return pl.BlockSpec((None, nm, dh), lambda b, t: (b, 0, off * heads + h))
        mem_specs = [spec(h, 0) for h in range(heads)] + [spec(h, voff) for h in range(heads)]
        mems = (mem_k,) * heads + (mem_v,) * heads
    return pl.pallas_call(
        functools.partial(_xattn_kernel, scale=float(dh) ** -0.5, heads=heads, dh=dh),
        grid=(batch, nt),
        in_specs=[pl.BlockSpec((tt, xw), lambda b, t: (b * nt + t, off_q // xw))] + mem_specs,
        out_specs=pl.BlockSpec((tt, xw), lambda b, t: (b * nt + t, 0)),
        out_shape=jax.ShapeDtypeStruct((m, xw), BF16),
        compiler_params=_cparams("parallel", "parallel"),
        name="xattn",
    )(z, *mems)


def _mix_kernel(og_ref, cg_ref, at_ref, m0_ref, m1_ref, m2_ref, bm_ref, wg_ref, wc_ref, wx_ref, o_ref):
    bm = bm_ref[...]
    yg = _dot(og_ref[...], wg_ref[...])
    yc = _dot(cg_ref[...], wc_ref[...])
    yx = _dot(at_ref[...], wx_ref[...])
    mixed = (jax.nn.sigmoid(m0_ref[...].astype(F32) + bm[0:1, :]) * yg
             + jax.nn.sigmoid(m1_ref[...].astype(F32) + bm[1:2, :]) * yc
             + jax.nn.sigmoid(m2_ref[...].astype(F32) + bm[2:3, :]) * yx)
    o_ref[...] = mixed.astype(o_ref.dtype)


def _mix(og, cg, at, z, bm, wg, wc, wx, *, off_m, tm, tn):
    m = og.shape[0]
    d = wg.shape[1]
    tm, tn = _tile(m, tm), _tile(d, tn)
    assert off_m % tn == 0
    nb = bm.shape[0]

    def res(a):
        return pl.BlockSpec((tm, a.shape[1]), lambda i, j: (i, 0))

    def wspec(a):
        return pl.BlockSpec((a.shape[0], tn), lambda i, j: (0, j))

    def mspec(k):
        return pl.BlockSpec((tm, tn), lambda i, j: (i, (off_m + k * d) // tn + j))

    return pl.pallas_call(
        _mix_kernel,
        grid=(m // tm, d // tn),
        in_specs=[res(og), res(cg), res(at), mspec(0), mspec(1), mspec(2),
                  pl.BlockSpec((nb, tn), lambda i, j: (0, j)),
                  wspec(wg), wspec(wc), wspec(wx)],
        out_specs=pl.BlockSpec((tm, tn), lambda i, j: (i, j)),
        out_shape=jax.ShapeDtypeStruct((m, d), BF16),
        compiler_params=_cparams("parallel", "arbitrary"),
        name="mix",
    )(og, cg, at, z, z, z, bm, wg, wc, wx)


def _outproj_kernel(a_ref, w_ref, x_ref, o_ref):
    o_ref[...] = x_ref[...] + _dot(a_ref[...], w_ref[...])


def _outproj(a, w, x, *, tm, tn):
    m, k = a.shape
    n = w.shape[1]
    tm, tn = _tile(m, tm), _tile(n, tn)
    return pl.pallas_call(
        _outproj_kernel,
        grid=(m // tm, n // tn),
        in_specs=[pl.BlockSpec((tm, k), lambda i, j: (i, 0)),
                  pl.BlockSpec((k, tn), lambda i, j: (0, j)),
                  pl.BlockSpec((tm, tn), lambda i, j: (i, j))],
        out_specs=pl.BlockSpec((tm, tn), lambda i, j: (i, j)),
        out_shape=jax.ShapeDtypeStruct((m, n), F32),
        compiler_params=_cparams("parallel", "arbitrary"),
        name="outproj",
    )(a, w, x)


def _ffn_kernel(x_ref, g_ref, wg_ref, wu_ref, wd_ref, gf_ref, o_ref, hn_ref, *, final_norm):
    f = pl.program_id(1)
    tm = x_ref.shape[0]

    @pl.when(f == 0)
    def _():
        g = g_ref[...]

        def init(sl):
            x = x_ref[sl, :]
            hn_ref[sl, :] = _rms(x, g).astype(BF16)
            o_ref[sl, :] = x

        _rows_loop(tm, init)

    hn = hn_ref[...]
    gate = _dot(hn, wg_ref[...])
    up = _dot(hn, wu_ref[...])
    h = ((gate * jax.nn.sigmoid(gate)) * up).astype(BF16)
    nb = min(FFN_NB, o_ref.shape[1])
    for n in range(o_ref.shape[1] // nb):
        cols = slice(n * nb, (n + 1) * nb)
        o_ref[:, cols] += _dot(h, wd_ref[:, cols])

    if final_norm:
        @pl.when(f == pl.num_programs(1) - 1)
        def _():
            gf = gf_ref[...]

            def fin(sl):
                o_ref[sl, :] = _rms(o_ref[sl, :], gf)

            _rows_loop(tm, fin)


def _ffn(x, g, wg, wu, wd, gf, *, final_norm, tm, tf):
    m, d = x.shape
    dff = wg.shape[1]
    tm, tf = _tile(m, tm), _tile(dff, tf)
    return pl.pallas_call(
        functools.partial(_ffn_kernel, final_norm=final_norm),
        grid=(m // tm, dff // tf),
        in_specs=[pl.BlockSpec((tm, d), lambda i, f: (i, 0), pipeline_mode=pl.Buffered(1)),
                  pl.BlockSpec((1, d), lambda i, f: (0, 0)),
                  pl.BlockSpec((d, tf), lambda i, f: (0, f)),
                  pl.BlockSpec((d, tf), lambda i, f: (0, f)),
                  pl.BlockSpec((tf, d), lambda i, f: (f, 0)),
                  pl.BlockSpec((1, d), lambda i, f: (0, 0))],
        out_specs=pl.BlockSpec((tm, d), lambda i, f: (i, 0)),
        out_shape=jax.ShapeDtypeStruct((m, d), F32),
        scratch_shapes=[pltpu.VMEM((tm, d), BF16)],
        compiler_params=_cparams("parallel", "arbitrary"),
        name="ffn",
    )(x, g, wg, wu, wd, gf)


GLA_CHUNK = 128
GLA_BODIES = 4
FFN_TF = 256
FFN_NB = 1024


def _layer(x, batch, seq, s0, buf, mem_k, mem_v, voff, p, dims, *, final_norm):
    d, gk, gv, cw, xw, heads, dk, dv, xh, dh = dims
    off_q, off_k, off_v, off_r = 0, gk, 2 * gk, 2 * gk + gv
    off_cb = off_r + gv
    off_xq = off_cb + 3 * cw
    off_m = off_xq + xw
    z, a_lr = _inproj(x, p["g_mix"], p["w_main"], p["w_alr"], tm=1024, tn=1024)
    tc = _tile(seq, GLA_CHUNK * GLA_BODIES)
    hb = max(1, GLA_BODIES // max(1, tc // GLA_CHUNK))
    og, s_new = _gla(z, a_lr, p["w_a2"], p["b_a"], p["g_gla_out"], s0, batch=batch, seq=seq, heads=heads,
                     dk=dk, dv=dv, off_q=off_q, off_k=off_k, off_v=off_v, off_r=off_r,
                     tc=tc, chunk=GLA_CHUNK, hb=hb)
    cg, new_buf = _conv(z, buf, p["w_conv"], batch=batch, seq=seq, cw=cw, off_cb=off_cb, tt=512)
    at = _xattn(z, mem_k, mem_v, batch=batch, seq=seq, heads=xh, dh=dh, off_q=off_xq, voff=voff, tt=1024)
    mixed = _mix(og, cg, at, z, p["b_merge"], p["w_gla_o"], p["w_conv_o"], p["w_xa_o"],
                 off_m=off_m, tm=1024, tn=1024)
    x1 = _outproj(mixed, p["w_out"], x, tm=1024, tn=1024)
    x2 = _ffn(x1, p["g_ffn"], p["w_ffn_gate"], p["w_ffn_up"], p["w_ffn_down"], p["g_final"],
              final_norm=final_norm, tm=512, tf=FFN_TF)
    return x2, s_new, new_buf


def kernel(x_prompt, x_sample, state_gla, cache_conv, cache_mem_k, cache_mem_v, mem_prompt, g_mix, w_in, w_a2, b_a, g_gla_out, w_gla_o, w_conv, w_conv_o, w_xa_o, g_mem, w_mem_kv, b_merge, w_out, g_ffn, w_ffn_gate, w_ffn_up, w_ffn_down, g_final):
    depth, bs, heads, dk, dv = state_gla.shape
    bp, tp, d = x_prompt.shape
    _, ts, _ = x_sample.shape
    gk, gv = heads * dk, heads * dv
    cw = cache_conv.shape[-1]
    _, _, n_mem, xh, dh = cache_mem_k.shape
    xw = xh * dh
    rank = w_a2.shape[1]
    dims = (d, gk, gv, cw, xw, heads, dk, dv, xh, dh)
    off_a = 2 * gk + gv
    assert rank <= LANES

    hp = x_prompt.reshape(bp * tp, d)
    hs = x_sample.reshape(bs * ts, d)
    row = lambda a: a.reshape(1, -1)
    outs = [[] for _ in range(6)]
    for l in range(depth):
        wt = jnp.transpose(w_in[l])
        p = {
            "g_mix": row(g_mix[l]),
            "w_main": _drop_rows_t(wt, off_a, rank, tn=1024, tk=1024),
            "w_alr": _rows_t_pad(wt, off_a, rank, tk=1024),
            "w_a2": jnp.pad(w_a2[l], ((0, LANES - rank), (0, 0))).astype(BF16),
            "b_a": row(b_a[l]),
            "g_gla_out": row(g_gla_out[l]),
            "w_gla_o": w_gla_o[l].astype(BF16),
            "w_conv": w_conv[l],
            "w_conv_o": w_conv_o[l].astype(BF16),
            "w_xa_o": w_xa_o[l].astype(BF16),
            "b_merge": b_merge[l],
            "w_out": w_out[l].astype(BF16),
            "g_ffn": row(g_ffn[l]),
            "w_ffn_gate": w_ffn_gate[l].astype(BF16),
            "w_ffn_up": w_ffn_up[l].astype(BF16),
            "w_ffn_down": w_ffn_down[l].astype(BF16),
            "g_final": row(g_final),
        }
        last = l == depth - 1
        kv = _norm_matmul(mem_prompt.reshape(bp * n_mem, d), row(g_mem[l]), w_mem_kv[l].astype(BF16),
                          tm=512, tn=512).reshape(bp, n_mem, 2 * xw)
        s0 = jnp.zeros((bp, heads, dk, dv), state_gla.dtype)
        buf0 = jnp.zeros((bp,) + cache_conv.shape[2:], cache_conv.dtype)
        hp, sp, bufp = _layer(hp, bp, tp, s0, buf0, kv, kv, 1, p, dims, final_norm=last)
        hs, ss, bufs = _layer(hs, bs, ts, state_gla[l], cache_conv[l], cache_mem_k[l], cache_mem_v[l], 0,
                              p, dims, final_norm=last)
        new = (sp, bufp, kv[..., :xw].reshape(bp, n_mem, xh, dh), kv[..., xw:].reshape(bp, n_mem, xh, dh),
               ss, bufs)
        for acc, val in zip(outs, new):
            acc.append(val)
    return (hp.reshape(bp, tp, d), hs.reshape(bs, ts, d)) + tuple(jnp.stack(o) for o in outs)
```

```python
import functools

import jax
import jax.numpy as jnp
from jax import lax
from jax.experimental import pallas as pl
from jax.experimental.pallas import tpu as pltpu

F32 = jnp.float32
BF16 = jnp.bfloat16

EPS = 1e-6
GLA_BLOCK = 16
GATE_NORM = 16.0
LANES = 128
V7X_VMEM_BYTES = 64 * 1024 * 1024
VMEM_LIMIT = V7X_VMEM_BYTES - 4 * 1024 * 1024


def _cparams(*sem):
    return pltpu.CompilerParams(dimension_semantics=sem, vmem_limit_bytes=VMEM_LIMIT)


def _tile(n, pref):
    if n <= pref:
        return n
    t = pref
    while n % t:
        t //= 2
    return t


def _rms(x, g):
    ms = jnp.mean(x * x, axis=-1, keepdims=True)
    return x * lax.rsqrt(ms + EPS) * g


def _dot(a, b):
    return jnp.dot(a, b, preferred_element_type=F32)


def _dot_nt(a, b):
    return lax.dot_general(a, b, (((1,), (1,)), ((), ())), preferred_element_type=F32)


def _dot_tn(a, b):
    return lax.dot_general(a, b, (((0,), (0,)), ((), ())), preferred_element_type=F32)


ROW_BLOCK = 32


def _rows_loop(n_rows, fn):
    rb = min(ROW_BLOCK, n_rows)

    def body(i, c):
        fn(pl.ds(pl.multiple_of(i * rb, rb), rb))
        return c

    lax.fori_loop(0, n_rows // rb, body, 0)


def _drop_rows_t_kernel(a_ref, b_ref, o_ref, *, n_plain, shift):
    j = pl.program_id(0)

    @pl.when(j < n_plain)
    def _():
        o_ref[...] = a_ref[...].T.astype(BF16)

    @pl.when(j >= n_plain)
    def _():
        rows = jnp.concatenate([a_ref[shift:, :], b_ref[...]], axis=0)
        o_ref[...] = rows.T.astype(BF16)


def _drop_rows_t(wt, start, width, *, tn, tk):
    rows, cols = wt.shape
    n_out = rows - width
    tn, tk = _tile(n_out, tn), _tile(cols, tk)
    assert start % tn == 0 and width % 8 == 0 and tn % width == 0
    return pl.pallas_call(
        functools.partial(_drop_rows_t_kernel, n_plain=start // tn, shift=width),
        grid=(n_out // tn, cols // tk),
        in_specs=[pl.BlockSpec((tn, tk), lambda j, k: (j, k)),
                  pl.BlockSpec((width, tk), lambda j, k: ((j + 1) * (tn // width), k))],
        out_specs=pl.BlockSpec((tk, tn), lambda j, k: (k, j)),
        out_shape=jax.ShapeDtypeStruct((cols, n_out), BF16),
        compiler_params=_cparams("parallel", "parallel"),
        name="wprep",
    )(wt, wt)


def _rows_t_pad_kernel(a_ref, o_ref):
    a = a_ref[...]
    full = jnp.concatenate([a, jnp.zeros((LANES - a.shape[0], a.shape[1]), a.dtype)], axis=0)
    o_ref[...] = full.T.astype(BF16)


def _rows_t_pad(wt, start, width, *, tk):
    cols = wt.shape[1]
    tk = _tile(cols, tk)
    assert start % width == 0 and width % 8 == 0 and width <= LANES
    return pl.pallas_call(
        _rows_t_pad_kernel,
        grid=(cols // tk,),
        in_specs=[pl.BlockSpec((width, tk), lambda k: (start // width, k))],
        out_specs=pl.BlockSpec((tk, LANES), lambda k: (k, 0)),
        out_shape=jax.ShapeDtypeStruct((cols, LANES), BF16),
        compiler_params=_cparams("parallel"),
        name="wprep_alr",
    )(wt)


def _inproj_kernel(x_ref, g_ref, w_ref, wa_ref, z_ref, a_ref, xn_ref):
    @pl.when(pl.program_id(1) == 0)
    def _():
        g = g_ref[...]

        def norm(sl):
            xn_ref[sl, :] = _rms(x_ref[sl, :], g).astype(BF16)

        _rows_loop(x_ref.shape[0], norm)
        a_ref[...] = _dot(xn_ref[...], wa_ref[...])

    z_ref[...] = _dot(xn_ref[...], w_ref[...]).astype(z_ref.dtype)


def _inproj(x, g, w, wa, *, tm, tn):
    m, d = x.shape
    n = w.shape[1]
    tm, tn = _tile(m, tm), _tile(n, tn)
    return pl.pallas_call(
        _inproj_kernel,
        grid=(m // tm, n // tn),
        in_specs=[
            pl.BlockSpec((tm, d), lambda i, j: (i, 0), pipeline_mode=pl.Buffered(1)),
            pl.BlockSpec((1, d), lambda i, j: (0, 0)),
            pl.BlockSpec((d, tn), lambda i, j: (0, j)),
            pl.BlockSpec((d, LANES), lambda i, j: (0, 0)),
        ],
        out_specs=[
            pl.BlockSpec((tm, tn), lambda i, j: (i, j)),
            pl.BlockSpec((tm, LANES), lambda i, j: (i, 0)),
        ],
        out_shape=[jax.ShapeDtypeStruct((m, n), BF16), jax.ShapeDtypeStruct((m, LANES), F32)],
        scratch_shapes=[pltpu.VMEM((tm, d), BF16)],
        compiler_params=_cparams("parallel", "arbitrary"),
        name="inproj",
    )(x, g, w, wa)


def _norm_matmul_kernel(x_ref, g_ref, w_ref, o_ref, xn_ref):
    @pl.when(pl.program_id(1) == 0)
    def _():
        g = g_ref[...]

        def norm(sl):
            xn_ref[sl, :] = _rms(x_ref[sl, :], g).astype(BF16)

        _rows_loop(x_ref.shape[0], norm)

    o_ref[...] = _dot(xn_ref[...], w_ref[...])


def _norm_matmul(x, g, w, *, tm, tn):
    m, d = x.shape
    n = w.shape[1]
    tm, tn = _tile(m, tm), _tile(n, tn)
    return pl.pallas_call(
        _norm_matmul_kernel,
        grid=(m // tm, n // tn),
        in_specs=[
            pl.BlockSpec((tm, d), lambda i, j: (i, 0)),
            pl.BlockSpec((1, d), lambda i, j: (0, 0)),
            pl.BlockSpec((d, tn), lambda i, j: (0, j)),
        ],
        out_specs=pl.BlockSpec((tm, tn), lambda i, j: (i, j)),
        out_shape=jax.ShapeDtypeStruct((m, n), F32),
        scratch_shapes=[pltpu.VMEM((tm, d), BF16)],
        compiler_params=_cparams("parallel", "arbitrary"),
        name="memkv",
    )(x, g, w)


def _chunk_cumsum(x, pos, c):
    s = 1
    while s < c:
        x = x + jnp.where(pos >= s, pltpu.roll(x, s, 0), 0.0)
        s *= 2
    return x


def _group_end(x, h):
    n, w = x.shape
    return jnp.concatenate(
        [jnp.broadcast_to(x[(g + 1) * h - 1:(g + 1) * h, :], (h, w)) for g in range(n // h)], axis=0)


def _gla_kernel(q_ref, k_ref, v_ref, r_ref, a_ref, wa2_ref, ba_ref, g_ref, s0_ref,
                og_ref, sout_ref, s_ref, *, chunk, scale, dk, dv):
    t = pl.program_id(2)

    @pl.when(t == 0)
    def _():
        s_ref[...] = s0_ref[0]

    tc = q_ref.shape[0]
    hb = s_ref.shape[0]
    c = chunk
    nch = tc // c
    pos = lax.broadcasted_iota(jnp.int32, (tc, dk), 0) & (c - 1)
    ri = lax.broadcasted_iota(jnp.int32, (c, c), 0)
    ci = lax.broadcasted_iota(jnp.int32, (c, c), 1)
    hs = []
    h = GLA_BLOCK
    while h <= c:
        hs.append(h)
        h *= 2
    blk = GLA_BLOCK.bit_length() - 1
    mask0 = (lax.shift_right_logical(ri, blk) == lax.shift_right_logical(ci, blk)) & (ci <= ri)
    masks = {}
    for h in hs[:-1]:
        sh = (2 * h).bit_length() - 1
        same = lax.shift_right_logical(ri, sh) == lax.shift_right_logical(ci, sh)
        masks[h] = same & ((ri & (2 * h - 1)) >= h) & ((ci & (2 * h - 1)) < h)

    g = g_ref[...]
    a_bf = a_ref[...].astype(BF16)

    for hh in range(hb):
        kcol = slice(hh * dk, (hh + 1) * dk)
        vcol = slice(hh * dv, (hh + 1) * dv)
        la = jax.nn.log_sigmoid(_dot(a_bf, wa2_ref[:, kcol]) + ba_ref[:, kcol]) / GATE_NORM
        beta = _chunk_cumsum(la, pos, c)
        q = q_ref[:, kcol].astype(F32) * scale
        k = k_ref[:, kcol].astype(F32)
        qf, kf = {}, {}
        for h in hs:
            e_h = _group_end(beta, h)
            if h == c:
                s_h = jnp.zeros_like(beta)
                e_c = e_h
            else:
                s_h = jnp.where(pos >= h, pltpu.roll(e_h, h, 0), 0.0)
            qf[h] = (q * jnp.exp(beta - s_h)).astype(BF16)
            kf[h] = (k * jnp.exp(e_h - beta)).astype(BF16)
            if h == GLA_BLOCK:
                k_in = (k * jnp.exp(s_h - beta)).astype(BF16)
        dec_rows = jnp.exp(e_c)

        a_mats, incs, vs = [], [], []
        for i in range(nch):
            sl = slice(i * c, (i + 1) * c)
            a_mat = jnp.where(mask0, _dot_nt(qf[GLA_BLOCK][sl], k_in[sl]), 0.0)
            for h in hs[:-1]:
                a_mat = a_mat + jnp.where(masks[h], _dot_nt(qf[h][sl], kf[h][sl]), 0.0)
            v = v_ref[sl, vcol]
            a_mats.append(a_mat.astype(BF16))
            vs.append(v)
            incs.append(_dot_tn(kf[c][sl], v))

        s_val = s_ref[hh]
        starts = []
        for i in range(nch):
            starts.append(s_val.astype(BF16))
            dec = jnp.transpose(dec_rows[i * c:i * c + 8, :])[:, 0:1]
            s_val = s_val * dec + incs[i]
        s_ref[hh] = s_val

        for i in range(nch):
            sl = slice(i * c, (i + 1) * c)
            o = _dot(a_mats[i], vs[i]) + _dot(qf[c][sl], starts[i])
            on = _rms(o, g)
            r = r_ref[sl, vcol].astype(F32)
            og_ref[sl, vcol] = (on * (r * jax.nn.sigmoid(r))).astype(og_ref.dtype)

    @pl.when(t == pl.num_programs(2) - 1)
    def _():
        sout_ref[0] = s_ref[...]


def _gla(z, a_lr, wa2, ba, g, s0, *, batch, seq, heads, dk, dv, off_q, off_k, off_v, off_r, tc, chunk, hb):
    m = z.shape[0]
    tc = _tile(seq, tc)
    chunk = min(chunk, tc)
    nt = seq // tc
    hb = min(hb, heads)
    wk, wv = hb * dk, hb * dv
    assert heads % hb == 0
    assert off_q % wk == 0 and off_k % wk == 0 and off_v % wv == 0 and off_r % wv == 0
    assert chunk % GLA_BLOCK == 0 and tc % chunk == 0

    def zspec(width, off):
        return pl.BlockSpec((tc, width), lambda b, h, t: (b * nt + t, off // width + h))

    return pl.pallas_call(
        functools.partial(_gla_kernel, chunk=chunk, scale=float(dk) ** -0.5, dk=dk, dv=dv),
        grid=(batch, heads // hb, nt),
        in_specs=[
            zspec(wk, off_q), zspec(wk, off_k), zspec(wv, off_v), zspec(wv, off_r),
            pl.BlockSpec((tc, LANES), lambda b, h, t: (b * nt + t, 0)),
            pl.BlockSpec((LANES, wk), lambda b, h, t: (0, h)),
            pl.BlockSpec((1, wk), lambda b, h, t: (0, h)),
            pl.BlockSpec((1, dv), lambda b, h, t: (0, 0)),
            pl.BlockSpec((1, hb, dk, dv), lambda b, h, t: (b, h, 0, 0)),
        ],
        out_specs=[
            pl.BlockSpec((tc, wv), lambda b, h, t: (b * nt + t, h)),
            pl.BlockSpec((1, hb, dk, dv), lambda b, h, t: (b, h, 0, 0)),
        ],
        out_shape=[jax.ShapeDtypeStruct((m, heads * dv), BF16),
                   jax.ShapeDtypeStruct((batch, heads, dk, dv), F32)],
        scratch_shapes=[pltpu.VMEM((hb, dk, dv), F32)],
        compiler_params=_cparams("parallel", "parallel", "arbitrary"),
        name="gla",
    )(z, z, z, z, a_lr, wa2, ba, g, s0)


def _conv_kernel(cb_ref, cc_ref, ch_ref, buf_ref, w_ref, cg_ref, nb_ref, tail_ref):
    t = pl.program_id(1)

    @pl.when(t == 0)
    def _():
        tail_ref[0:2, :] = buf_ref[0]

    tt = cc_ref.shape[0]
    u = cc_ref[...].astype(F32) * ch_ref[...].astype(F32)
    rows = lax.broadcasted_iota(jnp.int32, u.shape, 0)
    p0 = tail_ref[0:1, :]
    p1 = tail_ref[1:2, :]
    u1 = jnp.where(rows == 0, p1, pltpu.roll(u, 1, 0))
    u2 = jnp.where(rows == 0, p0, jnp.where(rows == 1, p1, pltpu.roll(u, 2, 0)))
    w = w_ref[...]
    conv = u2 * w[0:1, :] + u1 * w[1:2, :] + u * w[2:3, :]
    cg_ref[...] = (cb_ref[...].astype(F32) * conv).astype(cg_ref.dtype)
    last2 = u[tt - 2:tt, :]
    tail_ref[0:2, :] = last2

    @pl.when(t == pl.num_programs(1) - 1)
    def _():
        nb_ref[0] = last2


def _conv(z, buf, w, *, batch, seq, cw, off_cb, tt):
    m = z.shape[0]
    tt = _tile(seq, tt)
    nt = seq // tt
    assert off_cb % cw == 0 and tt >= 2
    kk = buf.shape[1]

    def zspec(k):
        return pl.BlockSpec((tt, cw), lambda b, t: (b * nt + t, off_cb // cw + k))

    return pl.pallas_call(
        _conv_kernel,
        grid=(batch, nt),
        in_specs=[zspec(0), zspec(1), zspec(2),
                  pl.BlockSpec((1, kk, cw), lambda b, t: (b, 0, 0)),
                  pl.BlockSpec((kk + 1, cw), lambda b, t: (0, 0))],
        out_specs=[pl.BlockSpec((tt, cw), lambda b, t: (b * nt + t, 0)),
                   pl.BlockSpec((1, kk, cw), lambda b, t: (b, 0, 0))],
        out_shape=[jax.ShapeDtypeStruct((m, cw), BF16), jax.ShapeDtypeStruct((batch, kk, cw), F32)],
        scratch_shapes=[pltpu.VMEM((8, cw), F32)],
        compiler_params=_cparams("parallel", "arbitrary"),
        name="conv",
    )(z, z, z, buf, w)


def _xattn_kernel(q_ref, *refs, scale, heads, dh):
    if len(refs) == 3:
        ks = pltpu.einshape("mhd->hmd", refs[0][...].astype(BF16))
        vs = pltpu.einshape("mhd->hmd", refs[1][...].astype(BF16))
        k_of, v_of = (lambda h: ks[h]), (lambda h: vs[h])
    else:
        k_of = lambda h: refs[h][...].astype(BF16)
        v_of = lambda h: refs[heads + h][...].astype(BF16)
    o_ref = refs[-1]
    for h in range(heads):
        col = slice(h * dh, (h + 1) * dh)
        s = _dot_nt(q_ref[:, col], k_of(h)) * scale
        p = jnp.exp(s - jnp.max(s, axis=-1, keepdims=True))
        p = p / jnp.sum(p, axis=-1, keepdims=True)
        o_ref[:, col] = _dot(p.astype(BF16), v_of(h)).astype(o_ref.dtype)


def _xattn(z, mem_k, mem_v, *, batch, seq, heads, dh, off_q, voff, tt):
    m = z.shape[0]
    tt = _tile(seq, tt)
    nt = seq // tt
    nm = mem_k.shape[1]
    xw = heads * dh
    assert off_q % xw == 0
    if mem_k.ndim == 4:
        mem_specs = [pl.BlockSpec((None, nm, heads, dh), lambda b, t: (b, 0, 0, 0))] * 2
        mems = (mem_k, mem_v)
    else:
        def spec(h, off):
            return pl.BlockSpec((None, nm, dh), lambda b, t: (b, 0, off * heads + h))
        mem_specs = [spec(h, 0) for h in range(heads)] + [spec(h, voff) for h in range(heads)]
        mems = (mem_k,) * heads + (mem_v,) * heads
    return pl.pallas_call(
        functools.partial(_xattn_kernel, scale=float(dh) ** -0.5, heads=heads, dh=dh),
        grid=(batch, nt),
        in_specs=[pl.BlockSpec((tt, xw), lambda b, t: (b * nt + t, off_q // xw))] + mem_specs,
        out_specs=pl.BlockSpec((tt, xw), lambda b, t: (b * nt + t, 0)),
        out_shape=jax.ShapeDtypeStruct((m, xw), BF16),
        compiler_params=_cparams("parallel", "parallel"),
        name="xattn",
    )(z, *mems)


def _mix_kernel(og_ref, cg_ref, at_ref, m0_ref, m1_ref, m2_ref, bm_ref, wg_ref, wc_ref, wx_ref, o_ref):
    bm = bm_ref[...]
    yg = _dot(og_ref[...], wg_ref[...])
    yc = _dot(cg_ref[...], wc_ref[...])
    yx = _dot(at_ref[...], wx_ref[...])
    mixed = (jax.nn.sigmoid(m0_ref[...].astype(F32) + bm[0:1, :]) * yg
             + jax.nn.sigmoid(m1_ref[...].astype(F32) + bm[1:2, :]) * yc
             + jax.nn.sigmoid(m2_ref[...].astype(F32) + bm[2:3, :]) * yx)
    o_ref[...] = mixed.astype(o_ref.dtype)


def _mix(og, cg, at, z, bm, wg, wc, wx, *, off_m, tm, tn):
    m = og.shape[0]
    d = wg.shape[1]
    tm, tn = _tile(m, tm), _tile(d, tn)
    assert off_m % tn == 0
    nb = bm.shape[0]

    def res(a):
        return pl.BlockSpec((tm, a.shape[1]), lambda i, j: (i, 0))

    def wspec(a):
        return pl.BlockSpec((a.shape[0], tn), lambda i, j: (0, j))

    def mspec(k):
        return pl.BlockSpec((tm, tn), lambda i, j: (i, (off_m + k * d) // tn + j))

    return pl.pallas_call(
        _mix_kernel,
        grid=(m // tm, d // tn),
        in_specs=[res(og), res(cg), res(at), mspec(0), mspec(1), mspec(2),
                  pl.BlockSpec((nb, tn), lambda i, j: (0, j)),
                  wspec(wg), wspec(wc), wspec(wx)],
        out_specs=pl.BlockSpec((tm, tn), lambda i, j: (i, j)),
        out_shape=jax.ShapeDtypeStruct((m, d), BF16),
        compiler_params=_cparams("parallel", "arbitrary"),
        name="mix",
    )(og, cg, at, z, z, z, bm, wg, wc, wx)


def _outproj_kernel(a_ref, w_ref, x_ref, o_ref):
    o_ref[...] = x_ref[...] + _dot(a_ref[...], w_ref[...])


def _outproj(a, w, x, *, tm, tn):
    m, k = a.shape
    n = w.shape[1]
    tm, tn = _tile(m, tm), _tile(n, tn)
    return pl.pallas_call(
        _outproj_kernel,
        grid=(m // tm, n // tn),
        in_specs=[pl.BlockSpec((tm, k), lambda i, j: (i, 0)),
                  pl.BlockSpec((k, tn), lambda i, j: (0, j)),
                  pl.BlockSpec((tm, tn), lambda i, j: (i, j))],
        out_specs=pl.BlockSpec((tm, tn), lambda i, j: (i, j)),
        out_shape=jax.ShapeDtypeStruct((m, n), F32),
        compiler_params=_cparams("parallel", "arbitrary"),
        name="outproj",
    )(a, w, x)


def _ffn_kernel(x_ref, g_ref, wg_ref, wu_ref, wd_ref, gf_ref, o_ref, hn_ref, *, final_norm):
    f = pl.program_id(1)
    tm = x_ref.shape[0]

    @pl.when(f == 0)
    def _():
        g = g_ref[...]

        def init(sl):
            x = x_ref[sl, :]
            hn_ref[sl, :] = _rms(x, g).astype(BF16)
            o_ref[sl, :] = x

        _rows_loop(tm, init)

    hn = hn_ref[...]
    gate = _dot(hn, wg_ref[...])
    up = _dot(hn, wu_ref[...])
    h = ((gate * jax.nn.sigmoid(gate)) * up).astype(BF16)
    nb = min(FFN_NB, o_ref.shape[1])
    for n in range(o_ref.shape[1] // nb):
        cols = slice(n * nb, (n + 1) * nb)
        o_ref[:, cols] += _dot(h, wd_ref[:, cols])

    if final_norm:
        @pl.when(f == pl.num_programs(1) - 1)
        def _():
            gf = gf_ref[...]

            def fin(sl):
                o_ref[sl, :] = _rms(o_ref[sl, :], gf)

            _rows_loop(tm, fin)


def _ffn(x, g, wg, wu, wd, gf, *, final_norm, tm, tf):
    m, d = x.shape
    dff = wg.shape[1]
    tm, tf = _tile(m, tm), _tile(dff, tf)
    return pl.pallas_call(
        functools.partial(_ffn_kernel, final_norm=final_norm),
        grid=(m // tm, dff // tf),
        in_specs=[pl.BlockSpec((tm, d), lambda i, f: (i, 0), pipeline_mode=pl.Buffered(1)),
                  pl.BlockSpec((1, d), lambda i, f: (0, 0)),
                  pl.BlockSpec((d, tf), lambda i, f: (0, f)),
                  pl.BlockSpec((d, tf), lambda i, f: (0, f)),
                  pl.BlockSpec((tf, d), lambda i, f: (f, 0)),
                  pl.BlockSpec((1, d), lambda i, f: (0, 0))],
        out_specs=pl.BlockSpec((tm, d), lambda i, f: (i, 0), pipeline_mode=pl.Buffered(1)),
        out_shape=jax.ShapeDtypeStruct((m, d), F32),
        scratch_shapes=[pltpu.VMEM((tm, d), BF16)],
        compiler_params=_cparams("parallel", "arbitrary"),
        name="ffn",
    )(x, g, wg, wu, wd, gf)


GLA_CHUNK = 128
GLA_BODIES = 4
FFN_TF = 256
FFN_NB = 1024


def _layer(x, batch, seq, s0, buf, mem_k, mem_v, voff, p, dims, *, final_norm):
    d, gk, gv, cw, xw, heads, dk, dv, xh, dh = dims
    off_q, off_k, off_v, off_r = 0, gk, 2 * gk, 2 * gk + gv
    off_cb = off_r + gv
    off_xq = off_cb + 3 * cw
    off_m = off_xq + xw
    z, a_lr = _inproj(x, p["g_mix"], p["w_main"], p["w_alr"], tm=1024, tn=1024)
    tc = _tile(seq, GLA_CHUNK * GLA_BODIES)
    hb = max(1, GLA_BODIES // max(1, tc // GLA_CHUNK))
    og, s_new = _gla(z, a_lr, p["w_a2"], p["b_a"], p["g_gla_out"], s0, batch=batch, seq=seq, heads=heads,
                     dk=dk, dv=dv, off_q=off_q, off_k=off_k, off_v=off_v, off_r=off_r,
                     tc=tc, chunk=GLA_CHUNK, hb=hb)
    cg, new_buf = _conv(z, buf, p["w_conv"], batch=batch, seq=seq, cw=cw, off_cb=off_cb, tt=512)
    at = _xattn(z, mem_k, mem_v, batch=batch, seq=seq, heads=xh, dh=dh, off_q=off_xq, voff=voff, tt=1024)
    mixed = _mix(og, cg, at, z, p["b_merge"], p["w_gla_o"], p["w_conv_o"], p["w_xa_o"],
                 off_m=off_m, tm=1024, tn=1024)
    x1 = _outproj(mixed, p["w_out"], x, tm=1024, tn=1024)
    x2 = _ffn(x1, p["g_ffn"], p["w_ffn_gate"], p["w_ffn_up"], p["w_ffn_down"], p["g_final"],
              final_norm=final_norm, tm=1024, tf=FFN_TF)
    return x2, s_new, new_buf


def kernel(x_prompt, x_sample, state_gla, cache_conv, cache_mem_k, cache_mem_v, mem_prompt, g_mix, w_in, w_a2, b_a, g_gla_out, w_gla_o, w_conv, w_conv_o, w_xa_o, g_mem, w_mem_kv, b_merge, w_out, g_ffn, w_ffn_gate, w_ffn_up, w_ffn_down, g_final):
    depth, bs, heads, dk, dv = state_gla.shape
    bp, tp, d = x_prompt.shape
    _, ts, _ = x_sample.shape
    gk, gv = heads * dk, heads * dv
    cw = cache_conv.shape[-1]
    _, _, n_mem, xh, dh = cache_mem_k.shape
    xw = xh * dh
    rank = w_a2.shape[1]
    dims = (d, gk, gv, cw, xw, heads, dk, dv, xh, dh)
    off_a = 2 * gk + gv
    assert rank <= LANES

    hp = x_prompt.reshape(bp * tp, d)
    hs = x_sample.reshape(bs * ts, d)
    row = lambda a: a.reshape(1, -1)
    outs = [[] for _ in range(6)]
    for l in range(depth):
        wt = jnp.transpose(w_in[l])
        p = {
            "g_mix": row(g_mix[l]),
            "w_main": _drop_rows_t(wt, off_a, rank, tn=1024, tk=1024),
            "w_alr": _rows_t_pad(wt, off_a, rank, tk=1024),
            "w_a2": jnp.pad(w_a2[l], ((0, LANES - rank), (0, 0))).astype(BF16),
            "b_a": row(b_a[l]),
            "g_gla_out": row(g_gla_out[l]),
            "w_gla_o": w_gla_o[l].astype(BF16),
            "w_conv": w_conv[l],
            "w_conv_o": w_conv_o[l].astype(BF16),
            "w_xa_o": w_xa_o[l].astype(BF16),
            "b_merge": b_merge[l],
            "w_out": w_out[l].astype(BF16),
            "g_ffn": row(g_ffn[l]),
            "w_ffn_gate": w_ffn_gate[l].astype(BF16),
            "w_ffn_up": w_ffn_up[l].astype(BF16),
            "w_ffn_down": w_ffn_down[l].astype(BF16),
            "g_final": row(g_final),
        }
        last = l == depth - 1
        kv = _norm_matmul(mem_prompt.reshape(bp * n_mem, d), row(g_mem[l]), w_mem_kv[l].astype(BF16),
                          tm=512, tn=512).reshape(bp, n_mem, 2 * xw)
        s0 = jnp.zeros((bp, heads, dk, dv), state_gla.dtype)
        buf0 = jnp.zeros((bp,) + cache_conv.shape[2:], cache_conv.dtype)
        hp, sp, bufp = _layer(hp, bp, tp, s0, buf0, kv, kv, 1, p, dims, final_norm=last)
        hs, ss, bufs = _layer(hs, bs, ts, state_gla[l], cache_conv[l], cache_mem_k[l], cache_mem_v[l], 0,
                              p, dims, final_norm=last)
        new = (sp, bufp, kv[..., :xw].reshape(bp, n_mem, xh, dh), kv[..., xw:].reshape(bp, n_mem, xh, dh),
               ss, bufs)
        for acc, val in zip(outs, new):
            acc.append(val)
    return (hp.reshape(bp, tp, d), hs.reshape(bs, ts, d)) + tuple(jnp.stack(o) for o in outs)
```

```python
import functools

import jax
import jax.numpy as jnp
from jax import lax
from jax.experimental import pallas as pl
from jax.experimental.pallas import tpu as pltpu

F32 = jnp.float32
BF16 = jnp.bfloat16

EPS = 1e-6
GLA_BLOCK = 16
GATE_NORM = 16.0
LANES = 128
V7X_VMEM_BYTES = 64 * 1024 * 1024
VMEM_LIMIT = V7X_VMEM_BYTES - 4 * 1024 * 1024


def _cparams(*sem):
    return pltpu.CompilerParams(dimension_semantics=sem, vmem_limit_bytes=VMEM_LIMIT)


def _tile(n, pref):
    if n <= pref:
        return n
    t = pref
    while n % t:
        t //= 2
    return t


def _rms(x, g):
    ms = jnp.mean(x * x, axis=-1, keepdims=True)
    return x * lax.rsqrt(ms + EPS) * g


def _dot(a, b):
    return jnp.dot(a, b, preferred_element_type=F32)


def _dot_nt(a, b):
    return lax.dot_general(a, b, (((1,), (1,)), ((), ())), preferred_element_type=F32)


def _dot_tn(a, b):
    return lax.dot_general(a, b, (((0,), (0,)), ((), ())), preferred_element_type=F32)


ROW_BLOCK = 32


def _rows_loop(n_rows, fn):
    rb = min(ROW_BLOCK, n_rows)

    def body(i, c):
        fn(pl.ds(pl.multiple_of(i * rb, rb), rb))
        return c

    lax.fori_loop(0, n_rows // rb, body, 0)


def _drop_rows_t_kernel(a_ref, b_ref, o_ref, *, n_plain, shift):
    j = pl.program_id(0)

    @pl.when(j < n_plain)
    def _():
        o_ref[...] = a_ref[...].T.astype(BF16)

    @pl.when(j >= n_plain)
    def _():
        rows = jnp.concatenate([a_ref[shift:, :], b_ref[...]], axis=0)
        o_ref[...] = rows.T.astype(BF16)


def _drop_rows_t(wt, start, width, *, tn, tk):
    rows, cols = wt.shape
    n_out = rows - width
    tn, tk = _tile(n_out, tn), _tile(cols, tk)
    assert start % tn == 0 and width % 8 == 0 and tn % width == 0
    return pl.pallas_call(
        functools.partial(_drop_rows_t_kernel, n_plain=start // tn, shift=width),
        grid=(n_out // tn, cols // tk),
        in_specs=[pl.BlockSpec((tn, tk), lambda j, k: (j, k)),
                  pl.BlockSpec((width, tk), lambda j, k: ((j + 1) * (tn // width), k))],
        out_specs=pl.BlockSpec((None, tk, tn), lambda j, k: (j, k, 0)),
        out_shape=jax.ShapeDtypeStruct((n_out // tn, cols, tn), BF16),
        compiler_params=_cparams("parallel", "parallel"),
        name="wprep",
    )(wt, wt)


def _rows_t_pad_kernel(a_ref, o_ref):
    a = a_ref[...]
    full = jnp.concatenate([a, jnp.zeros((LANES - a.shape[0], a.shape[1]), a.dtype)], axis=0)
    o_ref[...] = full.T.astype(BF16)


def _rows_t_pad(wt, start, width, *, tk):
    cols = wt.shape[1]
    tk = _tile(cols, tk)
    assert start % width == 0 and width % 8 == 0 and width <= LANES
    return pl.pallas_call(
        _rows_t_pad_kernel,
        grid=(cols // tk,),
        in_specs=[pl.BlockSpec((width, tk), lambda k: (start // width, k))],
        out_specs=pl.BlockSpec((tk, LANES), lambda k: (k, 0)),
        out_shape=jax.ShapeDtypeStruct((cols, LANES), BF16),
        compiler_params=_cparams("parallel"),
        name="wprep_alr",
    )(wt)


def _pair_tiles_kernel(a_ref, b_ref, o_ref):
    tn = a_ref.shape[1]
    o_ref[:, :tn] = a_ref[...].astype(BF16)
    o_ref[:, tn:] = b_ref[...].astype(BF16)


def _pair_tiles(a, b, tn):
    k, n = a.shape
    assert a.shape == b.shape and n % tn == 0
    return pl.pallas_call(
        _pair_tiles_kernel,
        grid=(n // tn,),
        in_specs=[pl.BlockSpec((k, tn), lambda j: (0, j))] * 2,
        out_specs=pl.BlockSpec((None, k, 2 * tn), lambda j: (j, 0, 0)),
        out_shape=jax.ShapeDtypeStruct((n // tn, k, 2 * tn), BF16),
        compiler_params=_cparams("parallel"),
        name="wprep_ffn",
    )(a, b)


def _inproj_kernel(x_ref, g_ref, w_ref, wa_ref, z_ref, a_ref, xn_ref):
    @pl.when(pl.program_id(1) == 0)
    def _():
        g = g_ref[...]

        def norm(sl):
            xn_ref[sl, :] = _rms(x_ref[sl, :], g).astype(BF16)

        _rows_loop(x_ref.shape[0], norm)
        a_ref[...] = _dot(xn_ref[...], wa_ref[...])

    z_ref[...] = _dot(xn_ref[...], w_ref[...]).astype(z_ref.dtype)


def _inproj(x, g, w, wa, *, tm):
    m, d = x.shape
    nt, _, tn = w.shape
    n = nt * tn
    tm = _tile(m, tm)
    return pl.pallas_call(
        _inproj_kernel,
        grid=(m // tm, nt),
        in_specs=[
            pl.BlockSpec((tm, d), lambda i, j: (i, 0), pipeline_mode=pl.Buffered(1)),
            pl.BlockSpec((1, d), lambda i, j: (0, 0)),
            pl.BlockSpec((None, d, tn), lambda i, j: (j, 0, 0)),
            pl.BlockSpec((d, LANES), lambda i, j: (0, 0)),
        ],
        out_specs=[
            pl.BlockSpec((tm, tn), lambda i, j: (i, j)),
            pl.BlockSpec((tm, LANES), lambda i, j: (i, 0)),
        ],
        out_shape=[jax.ShapeDtypeStruct((m, n), BF16), jax.ShapeDtypeStruct((m, LANES), F32)],
        scratch_shapes=[pltpu.VMEM((tm, d), BF16)],
        compiler_params=_cparams("parallel", "arbitrary"),
        name="inproj",
    )(x, g, w, wa)


def _norm_matmul_kernel(x_ref, g_ref, w_ref, o_ref, xn_ref):
    @pl.when(pl.program_id(1) == 0)
    def _():
        g = g_ref[...]

        def norm(sl):
            xn_ref[sl, :] = _rms(x_ref[sl, :], g).astype(BF16)

        _rows_loop(x_ref.shape[0], norm)

    o_ref[...] = _dot(xn_ref[...], w_ref[...])


def _norm_matmul(x, g, w, *, tm, tn):
    m, d = x.shape
    n = w.shape[1]
    tm, tn = _tile(m, tm), _tile(n, tn)
    return pl.pallas_call(
        _norm_matmul_kernel,
        grid=(m // tm, n // tn),
        in_specs=[
            pl.BlockSpec((tm, d), lambda i, j: (i, 0)),
            pl.BlockSpec((1, d), lambda i, j: (0, 0)),
            pl.BlockSpec((d, tn), lambda i, j: (0, j)),
        ],
        out_specs=pl.BlockSpec((tm, tn), lambda i, j: (i, j)),
        out_shape=jax.ShapeDtypeStruct((m, n), F32),
        scratch_shapes=[pltpu.VMEM((tm, d), BF16)],
        compiler_params=_cparams("parallel", "arbitrary"),
        name="memkv",
    )(x, g, w)


def _chunk_cumsum(x, pos, c):
    s = 1
    while s < c:
        x = x + jnp.where(pos >= s, pltpu.roll(x, s, 0), 0.0)
        s *= 2
    return x


def _group_end(x, h):
    n, w = x.shape
    return jnp.concatenate(
        [jnp.broadcast_to(x[(g + 1) * h - 1:(g + 1) * h, :], (h, w)) for g in range(n // h)], axis=0)


def _gla_kernel(q_ref, k_ref, v_ref, r_ref, a_ref, wa2_ref, ba_ref, g_ref, s0_ref,
                og_ref, sout_ref, s_ref, *, chunk, scale, dk, dv):
    t = pl.program_id(2)

    @pl.when(t == 0)
    def _():
        s_ref[...] = s0_ref[0]

    tc = q_ref.shape[0]
    hb = s_ref.shape[0]
    c = chunk
    nch = tc // c
    pos = lax.broadcasted_iota(jnp.int32, (tc, dk), 0) & (c - 1)
    ri = lax.broadcasted_iota(jnp.int32, (c, c), 0)
    ci = lax.broadcasted_iota(jnp.int32, (c, c), 1)
    hs = []
    h = GLA_BLOCK
    while h <= c:
        hs.append(h)
        h *= 2
    blk = GLA_BLOCK.bit_length() - 1
    mask0 = (lax.shift_right_logical(ri, blk) == lax.shift_right_logical(ci, blk)) & (ci <= ri)
    masks = {}
    for h in hs[:-1]:
        sh = (2 * h).bit_length() - 1
        same = lax.shift_right_logical(ri, sh) == lax.shift_right_logical(ci, sh)
        masks[h] = same & ((ri & (2 * h - 1)) >= h) & ((ci & (2 * h - 1)) < h)

    g = g_ref[...]
    a_bf = a_ref[...].astype(BF16)

    for hh in range(hb):
        kcol = slice(hh * dk, (hh + 1) * dk)
        vcol = slice(hh * dv, (hh + 1) * dv)
        la = jax.nn.log_sigmoid(_dot(a_bf, wa2_ref[:, kcol]) + ba_ref[:, kcol]) / GATE_NORM
        beta = _chunk_cumsum(la, pos, c)
        q = q_ref[:, kcol].astype(F32) * scale
        k = k_ref[:, kcol].astype(F32)
        qf, kf = {}, {}
        for h in hs:
            e_h = _group_end(beta, h)
            if h == c:
                s_h = jnp.zeros_like(beta)
                e_c = e_h
            else:
                s_h = jnp.where(pos >= h, pltpu.roll(e_h, h, 0), 0.0)
            qf[h] = (q * jnp.exp(beta - s_h)).astype(BF16)
            kf[h] = (k * jnp.exp(e_h - beta)).astype(BF16)
            if h == GLA_BLOCK:
                k_in = (k * jnp.exp(s_h - beta)).astype(BF16)
        dec_rows = jnp.exp(e_c)

        a_mats, incs, vs = [], [], []
        for i in range(nch):
            sl = slice(i * c, (i + 1) * c)
            a_mat = jnp.where(mask0, _dot_nt(qf[GLA_BLOCK][sl], k_in[sl]), 0.0)
            for h in hs[:-1]:
                a_mat = a_mat + jnp.where(masks[h], _dot_nt(qf[h][sl], kf[h][sl]), 0.0)
            v = v_ref[sl, vcol]
            a_mats.append(a_mat.astype(BF16))
            vs.append(v)
            incs.append(_dot_tn(kf[c][sl], v))

        s_val = s_ref[hh]
        starts = []
        for i in range(nch):
            starts.append(s_val.astype(BF16))
            dec = jnp.transpose(dec_rows[i * c:i * c + 8, :])[:, 0:1]
            s_val = s_val * dec + incs[i]
        s_ref[hh] = s_val

        for i in range(nch):
            sl = slice(i * c, (i + 1) * c)
            o = _dot(a_mats[i], vs[i]) + _dot(qf[c][sl], starts[i])
            on = _rms(o, g)
            r = r_ref[sl, vcol].astype(F32)
            og_ref[sl, vcol] = (on * (r * jax.nn.sigmoid(r))).astype(og_ref.dtype)

    @pl.when(t == pl.num_programs(2) - 1)
    def _():
        sout_ref[0] = s_ref[...]


def _gla(z, a_lr, wa2, ba, g, s0, *, batch, seq, heads, dk, dv, off_q, off_k, off_v, off_r, tc, chunk, hb):
    m = z.shape[0]
    tc = _tile(seq, tc)
    chunk = min(chunk, tc)
    nt = seq // tc
    hb = min(hb, heads)
    wk, wv = hb * dk, hb * dv
    assert heads % hb == 0
    assert off_q % wk == 0 and off_k % wk == 0 and off_v % wv == 0 and off_r % wv == 0
    assert chunk % GLA_BLOCK == 0 and tc % chunk == 0

    def zspec(width, off):
        return pl.BlockSpec((tc, width), lambda b, h, t: (b * nt + t, off // width + h))

    return pl.pallas_call(
        functools.partial(_gla_kernel, chunk=chunk, scale=float(dk) ** -0.5, dk=dk, dv=dv),
        grid=(batch, heads // hb, nt),
        in_specs=[
            zspec(wk, off_q), zspec(wk, off_k), zspec(wv, off_v), zspec(wv, off_r),
            pl.BlockSpec((tc, LANES), lambda b, h, t: (b * nt + t, 0)),
            pl.BlockSpec((LANES, wk), lambda b, h, t: (0, h)),
            pl.BlockSpec((1, wk), lambda b, h, t: (0, h)),
            pl.BlockSpec((1, dv), lambda b, h, t: (0, 0)),
            pl.BlockSpec((1, hb, dk, dv), lambda b, h, t: (b, h, 0, 0)),
        ],
        out_specs=[
            pl.BlockSpec((tc, wv), lambda b, h, t: (b * nt + t, h)),
            pl.BlockSpec((1, hb, dk, dv), lambda b, h, t: (b, h, 0, 0)),
        ],
        out_shape=[jax.ShapeDtypeStruct((m, heads * dv), BF16),
                   jax.ShapeDtypeStruct((batch, heads, dk, dv), F32)],
        scratch_shapes=[pltpu.VMEM((hb, dk, dv), F32)],
        compiler_params=_cparams("parallel", "parallel", "arbitrary"),
        name="gla",
    )(z, z, z, z, a_lr, wa2, ba, g, s0)


def _conv_kernel(cb_ref, cc_ref, ch_ref, buf_ref, w_ref, cg_ref, nb_ref, tail_ref):
    t = pl.program_id(1)

    @pl.when(t == 0)
    def _():
        tail_ref[0:2, :] = buf_ref[0]

    tt = cc_ref.shape[0]
    u = cc_ref[...].astype(F32) * ch_ref[...].astype(F32)
    rows = lax.broadcasted_iota(jnp.int32, u.shape, 0)
    p0 = tail_ref[0:1, :]
    p1 = tail_ref[1:2, :]
    u1 = jnp.where(rows == 0, p1, pltpu.roll(u, 1, 0))
    u2 = jnp.where(rows == 0, p0, jnp.where(rows == 1, p1, pltpu.roll(u, 2, 0)))
    w = w_ref[...]
    conv = u2 * w[0:1, :] + u1 * w[1:2, :] + u * w[2:3, :]
    cg_ref[...] = (cb_ref[...].astype(F32) * conv).astype(cg_ref.dtype)
    last2 = u[tt - 2:tt, :]
    tail_ref[0:2, :] = last2

    @pl.when(t == pl.num_programs(1) - 1)
    def _():
        nb_ref[0] = last2


def _conv(z, buf, w, *, batch, seq, cw, off_cb, tt):
    m = z.shape[0]
    tt = _tile(seq, tt)
    nt = seq // tt
    assert off_cb % cw == 0 and tt >= 2
    kk = buf.shape[1]

    def zspec(k):
        return pl.BlockSpec((tt, cw), lambda b, t: (b * nt + t, off_cb // cw + k))

    return pl.pallas_call(
        _conv_kernel,
        grid=(batch, nt),
        in_specs=[zspec(0), zspec(1), zspec(2),
                  pl.BlockSpec((1, kk, cw), lambda b, t: (b, 0, 0)),
                  pl.BlockSpec((kk + 1, cw), lambda b, t: (0, 0))],
        out_specs=[pl.BlockSpec((tt, cw), lambda b, t: (b * nt + t, 0)),
                   pl.BlockSpec((1, kk, cw), lambda b, t: (b, 0, 0))],
        out_shape=[jax.ShapeDtypeStruct((m, cw), BF16), jax.ShapeDtypeStruct((batch, kk, cw), F32)],
        scratch_shapes=[pltpu.VMEM((8, cw), F32)],
        compiler_params=_cparams("parallel", "arbitrary"),
        name="conv",
    )(z, z, z, buf, w)


def _xattn_kernel(q_ref, *refs, scale, heads, dh):
    if len(refs) == 3:
        ks = pltpu.einshape("mhd->hmd", refs[0][...].astype(BF16))
        vs = pltpu.einshape("mhd->hmd", refs[1][...].astype(BF16))
        k_of, v_of = (lambda h: ks[h]), (lambda h: vs[h])
    else:
        k_of = lambda h: refs[h][...].astype(BF16)
        v_of = lambda h: refs[heads + h][...].astype(BF16)
    o_ref = refs[-1]
    for h in range(heads):
        col = slice(h * dh, (h + 1) * dh)
        s = _dot_nt(q_ref[:, col], k_of(h)) * scale
        p = jnp.exp(s - jnp.max(s, axis=-1, keepdims=True))
        p = p / jnp.sum(p, axis=-1, keepdims=True)
        o_ref[:, col] = _dot(p.astype(BF16), v_of(h)).astype(o_ref.dtype)


def _xattn(z, mem_k, mem_v, *, batch, seq, heads, dh, off_q, voff, tt):
    m = z.shape[0]
    tt = _tile(seq, tt)
    nt = seq // tt
    nm = mem_k.shape[1]
    xw = heads * dh
    assert off_q % xw == 0
    if mem_k.ndim == 4:
        mem_specs = [pl.BlockSpec((None, nm, heads, dh), lambda b, t: (b, 0, 0, 0))] * 2
        mems = (mem_k, mem_v)
    else:
        def spec(h, off):
            return pl.BlockSpec((None, nm, dh), lambda b, t: (b, 0, off * heads + h))
        mem_specs = [spec(h, 0) for h in range(heads)] + [spec(h, voff) for h in range(heads)]
        mems = (mem_k,) * heads + (mem_v,) * heads
    return pl.pallas_call(
        functools.partial(_xattn_kernel, scale=float(dh) ** -0.5, heads=heads, dh=dh),
        grid=(batch, nt),
        in_specs=[pl.BlockSpec((tt, xw), lambda b, t: (b * nt + t, off_q // xw))] + mem_specs,
        out_specs=pl.BlockSpec((tt, xw), lambda b, t: (b * nt + t, 0)),
        out_shape=jax.ShapeDtypeStruct((m, xw), BF16),
        compiler_params=_cparams("parallel", "parallel"),
        name="xattn",
    )(z, *mems)


def _mix_kernel(og_ref, cg_ref, at_ref, m0_ref, m1_ref, m2_ref, bm_ref, wg_ref, wc_ref, wx_ref, o_ref):
    bm = bm_ref[...]
    yg = _dot(og_ref[...], wg_ref[...])
    yc = _dot(cg_ref[...], wc_ref[...])
    yx = _dot(at_ref[...], wx_ref[...])
    mixed = (jax.nn.sigmoid(m0_ref[...].astype(F32) + bm[0:1, :]) * yg
             + jax.nn.sigmoid(m1_ref[...].astype(F32) + bm[1:2, :]) * yc
             + jax.nn.sigmoid(m2_ref[...].astype(F32) + bm[2:3, :]) * yx)
    o_ref[...] = mixed.astype(o_ref.dtype)


def _mix(og, cg, at, z, bm, wg, wc, wx, *, off_m, tm, tn):
    m = og.shape[0]
    d = wg.shape[1]
    tm, tn = _tile(m, tm), _tile(d, tn)
    assert off_m % tn == 0
    nb = bm.shape[0]

    def res(a):
        return pl.BlockSpec((tm, a.shape[1]), lambda i, j: (i, 0))

    def wspec(a):
        return pl.BlockSpec((a.shape[0], tn), lambda i, j: (0, j))

    def mspec(k):
        return pl.BlockSpec((tm, tn), lambda i, j: (i, (off_m + k * d) // tn + j))

    return pl.pallas_call(
        _mix_kernel,
        grid=(m // tm, d // tn),
        in_specs=[res(og), res(cg), res(at), mspec(0), mspec(1), mspec(2),
                  pl.BlockSpec((nb, tn), lambda i, j: (0, j)),
                  wspec(wg), wspec(wc), wspec(wx)],
        out_specs=pl.BlockSpec((tm, tn), lambda i, j: (i, j)),
        out_shape=jax.ShapeDtypeStruct((m, d), BF16),
        compiler_params=_cparams("parallel", "arbitrary"),
        name="mix",
    )(og, cg, at, z, z, z, bm, wg, wc, wx)


def _outproj_kernel(a_ref, w_ref, x_ref, o_ref):
    o_ref[...] = x_ref[...] + _dot(a_ref[...], w_ref[...])


def _outproj(a, w, x, *, tm, tn):
    m, k = a.shape
    n = w.shape[1]
    tm, tn = _tile(m, tm), _tile(n, tn)
    return pl.pallas_call(
        _outproj_kernel,
        grid=(m // tm, n // tn),
        in_specs=[pl.BlockSpec((tm, k), lambda i, j: (i, 0)),
                  pl.BlockSpec((k, tn), lambda i, j: (0, j)),
                  pl.BlockSpec((tm, tn), lambda i, j: (i, j))],
        out_specs=pl.BlockSpec((tm, tn), lambda i, j: (i, j)),
        out_shape=jax.ShapeDtypeStruct((m, n), F32),
        compiler_params=_cparams("parallel", "arbitrary"),
        name="outproj",
    )(a, w, x)


def _ffn_kernel(x_ref, g_ref, wgu_ref, wd_ref, gf_ref, o_ref, hn_ref, *, final_norm):
    f = pl.program_id(1)
    tm = x_ref.shape[0]

    @pl.when(f == 0)
    def _():
        g = g_ref[...]

        def init(sl):
            x = x_ref[sl, :]
            hn_ref[sl, :] = _rms(x, g).astype(BF16)
            o_ref[sl, :] = x

        _rows_loop(tm, init)

    tf = wd_ref.shape[0]
    gu = _dot(hn_ref[...], wgu_ref[...])
    gate, up = gu[:, :tf], gu[:, tf:]
    h = ((gate * jax.nn.sigmoid(gate)) * up).astype(BF16)
    nb = min(FFN_NB, o_ref.shape[1])
    for n in range(o_ref.shape[1] // nb):
        cols = slice(n * nb, (n + 1) * nb)
        o_ref[:, cols] += _dot(h, wd_ref[:, cols])

    if final_norm:
        @pl.when(f == pl.num_programs(1) - 1)
        def _():
            gf = gf_ref[...]

            def fin(sl):
                o_ref[sl, :] = _rms(o_ref[sl, :], gf)

            _rows_loop(tm, fin)


def _ffn(x, g, wgu, wd, gf, *, final_norm, tm):
    m, d = x.shape
    nf, _, tf2 = wgu.shape
    tf = tf2 // 2
    tm = _tile(m, tm)
    return pl.pallas_call(
        functools.partial(_ffn_kernel, final_norm=final_norm),
        grid=(m // tm, nf),
        in_specs=[pl.BlockSpec((tm, d), lambda i, f: (i, 0), pipeline_mode=pl.Buffered(1)),
                  pl.BlockSpec((1, d), lambda i, f: (0, 0)),
                  pl.BlockSpec((None, d, tf2), lambda i, f: (f, 0, 0)),
                  pl.BlockSpec((tf, d), lambda i, f: (f, 0)),
                  pl.BlockSpec((1, d), lambda i, f: (0, 0))],
        out_specs=pl.BlockSpec((tm, d), lambda i, f: (i, 0), pipeline_mode=pl.Buffered(1)),
        out_shape=jax.ShapeDtypeStruct((m, d), F32),
        scratch_shapes=[pltpu.VMEM((tm, d), BF16)],
        compiler_params=_cparams("parallel", "arbitrary"),
        name="ffn",
    )(x, g, wgu, wd, gf)


GLA_CHUNK = 128
GLA_BODIES = 4
FFN_TF = 256
FFN_NB = 1024


def _layer(x, batch, seq, s0, buf, mem_k, mem_v, voff, p, dims, *, final_norm):
    d, gk, gv, cw, xw, heads, dk, dv, xh, dh = dims
    off_q, off_k, off_v, off_r = 0, gk, 2 * gk, 2 * gk + gv
    off_cb = off_r + gv
    off_xq = off_cb + 3 * cw
    off_m = off_xq + xw
    z, a_lr = _inproj(x, p["g_mix"], p["w_main"], p["w_alr"], tm=1024)
    tc = _tile(seq, GLA_CHUNK * GLA_BODIES)
    hb = max(1, GLA_BODIES // max(1, tc // GLA_CHUNK))
    og, s_new = _gla(z, a_lr, p["w_a2"], p["b_a"], p["g_gla_out"], s0, batch=batch, seq=seq, heads=heads,
                     dk=dk, dv=dv, off_q=off_q, off_k=off_k, off_v=off_v, off_r=off_r,
                     tc=tc, chunk=GLA_CHUNK, hb=hb)
    cg, new_buf = _conv(z, buf, p["w_conv"], batch=batch, seq=seq, cw=cw, off_cb=off_cb, tt=512)
    at = _xattn(z, mem_k, mem_v, batch=batch, seq=seq, heads=xh, dh=dh, off_q=off_xq, voff=voff, tt=1024)
    mixed = _mix(og, cg, at, z, p["b_merge"], p["w_gla_o"], p["w_conv_o"], p["w_xa_o"],
                 off_m=off_m, tm=1024, tn=1024)
    x1 = _outproj(mixed, p["w_out"], x, tm=1024, tn=1024)
    x2 = _ffn(x1, p["g_ffn"], p["w_ffn_gu"], p["w_ffn_down"], p["g_final"], final_norm=final_norm, tm=1024)
    return x2, s_new, new_buf


def kernel(x_prompt, x_sample, state_gla, cache_conv, cache_mem_k, cache_mem_v, mem_prompt, g_mix, w_in, w_a2, b_a, g_gla_out, w_gla_o, w_conv, w_conv_o, w_xa_o, g_mem, w_mem_kv, b_merge, w_out, g_ffn, w_ffn_gate, w_ffn_up, w_ffn_down, g_final):
    depth, bs, heads, dk, dv = state_gla.shape
    bp, tp, d = x_prompt.shape
    _, ts, _ = x_sample.shape
    gk, gv = heads * dk, heads * dv
    cw = cache_conv.shape[-1]
    _, _, n_mem, xh, dh = cache_mem_k.shape
    xw = xh * dh
    rank = w_a2.shape[1]
    dims = (d, gk, gv, cw, xw, heads, dk, dv, xh, dh)
    off_a = 2 * gk + gv
    assert rank <= LANES
    tf = _tile(w_ffn_gate.shape[-1], FFN_TF)

    hp = x_prompt.reshape(bp * tp, d)
    hs = x_sample.reshape(bs * ts, d)
    row = lambda a: a.reshape(1, -1)
    outs = [[] for _ in range(6)]
    for l in range(depth):
        wt = jnp.transpose(w_in[l])
        p = {
            "g_mix": row(g_mix[l]),
            "w_main": _drop_rows_t(wt, off_a, rank, tn=1024, tk=1024),
            "w_alr": _rows_t_pad(wt, off_a, rank, tk=1024),
            "w_a2": jnp.pad(w_a2[l], ((0, LANES - rank), (0, 0))).astype(BF16),
            "b_a": row(b_a[l]),
            "g_gla_out": row(g_gla_out[l]),
            "w_gla_o": w_gla_o[l].astype(BF16),
            "w_conv": w_conv[l],
            "w_conv_o": w_conv_o[l].astype(BF16),
            "w_xa_o": w_xa_o[l].astype(BF16),
            "b_merge": b_merge[l],
            "w_out": w_out[l].astype(BF16),
            "g_ffn": row(g_ffn[l]),
            "w_ffn_gu": _pair_tiles(w_ffn_gate[l], w_ffn_up[l], tf),
            "w_ffn_down": w_ffn_down[l].astype(BF16),
            "g_final": row(g_final),
        }
        last = l == depth - 1
        kv = _norm_matmul(mem_prompt.reshape(bp * n_mem, d), row(g_mem[l]), w_mem_kv[l].astype(BF16),
                          tm=512, tn=512).reshape(bp, n_mem, 2 * xw)
        s0 = jnp.zeros((bp, heads, dk, dv), state_gla.dtype)
        buf0 = jnp.zeros((bp,) + cache_conv.shape[2:], cache_conv.dtype)
        hp, sp, bufp = _layer(hp, bp, tp, s0, buf0, kv, kv, 1, p, dims, final_norm=last)
        hs, ss, bufs = _layer(hs, bs, ts, state_gla[l], cache_conv[l], cache_mem_k[l], cache_mem_v[l], 0,
                              p, dims, final_norm=last)
        new = (sp, bufp, kv[..., :xw].reshape(bp, n_mem, xh, dh), kv[..., xw:].reshape(bp, n_mem, xh, dh),
               ss, bufs)
        for acc, val in zip(outs, new):
            acc.append(val)
    return (hp.reshape(bp, tp, d), hs.reshape(bs, ts, d)) + tuple(jnp.stack(o) for o in outs)
```

```python
import functools

import jax
import jax.numpy as jnp
from jax import lax
from jax.experimental import pallas as pl
from jax.experimental.pallas import tpu as pltpu

F32 = jnp.float32
BF16 = jnp.bfloat16

EPS = 1e-6
GLA_BLOCK = 16
GATE_NORM = 16.0
LANES = 128
V7X_VMEM_BYTES = 64 * 1024 * 1024
VMEM_LIMIT = V7X_VMEM_BYTES - 4 * 1024 * 1024


def _cparams(*sem):
    return pltpu.CompilerParams(dimension_semantics=sem, vmem_limit_bytes=VMEM_LIMIT)


def _tile(n, pref):
    if n <= pref:
        return n
    t = pref
    while n % t:
        t //= 2
    return t


def _rms(x, g):
    ms = jnp.mean(x * x, axis=-1, keepdims=True)
    return x * lax.rsqrt(ms + EPS) * g


def _dot(a, b):
    return jnp.dot(a, b, preferred_element_type=F32)


def _dot_nt(a, b):
    return lax.dot_general(a, b, (((1,), (1,)), ((), ())), preferred_element_type=F32)


def _dot_tn(a, b):
    return lax.dot_general(a, b, (((0,), (0,)), ((), ())), preferred_element_type=F32)


ROW_BLOCK = 32


def _rows_loop(n_rows, fn):
    rb = min(ROW_BLOCK, n_rows)

    def body(i, c):
        fn(pl.ds(pl.multiple_of(i * rb, rb), rb))
        return c

    lax.fori_loop(0, n_rows // rb, body, 0)


def _drop_rows_t_kernel(a_ref, b_ref, o_ref, *, n_plain, shift):
    j = pl.program_id(0)

    @pl.when(j < n_plain)
    def _():
        o_ref[...] = a_ref[...].T.astype(BF16)

    @pl.when(j >= n_plain)
    def _():
        rows = jnp.concatenate([a_ref[shift:, :], b_ref[...]], axis=0)
        o_ref[...] = rows.T.astype(BF16)


def _drop_rows_t(wt, start, width, *, tn, tk):
    rows, cols = wt.shape
    n_out = rows - width
    tn, tk = _tile(n_out, tn), _tile(cols, tk)
    assert start % tn == 0 and width % 8 == 0 and tn % width == 0
    return pl.pallas_call(
        functools.partial(_drop_rows_t_kernel, n_plain=start // tn, shift=width),
        grid=(n_out // tn, cols // tk),
        in_specs=[pl.BlockSpec((tn, tk), lambda j, k: (j, k)),
                  pl.BlockSpec((width, tk), lambda j, k: ((j + 1) * (tn // width), k))],
        out_specs=pl.BlockSpec((None, tk, tn), lambda j, k: (j, k, 0)),
        out_shape=jax.ShapeDtypeStruct((n_out // tn, cols, tn), BF16),
        compiler_params=_cparams("parallel", "parallel"),
        name="wprep",
    )(wt, wt)


def _rows_t_pad_kernel(a_ref, o_ref):
    a = a_ref[...]
    full = jnp.concatenate([a, jnp.zeros((LANES - a.shape[0], a.shape[1]), a.dtype)], axis=0)
    o_ref[...] = full.T.astype(BF16)


def _rows_t_pad(wt, start, width, *, tk):
    cols = wt.shape[1]
    tk = _tile(cols, tk)
    assert start % width == 0 and width % 8 == 0 and width <= LANES
    return pl.pallas_call(
        _rows_t_pad_kernel,
        grid=(cols // tk,),
        in_specs=[pl.BlockSpec((width, tk), lambda k: (start // width, k))],
        out_specs=pl.BlockSpec((tk, LANES), lambda k: (k, 0)),
        out_shape=jax.ShapeDtypeStruct((cols, LANES), BF16),
        compiler_params=_cparams("parallel"),
        name="wprep_alr",
    )(wt)


def _pair_tiles_kernel(a_ref, b_ref, o_ref):
    tn = a_ref.shape[1]
    o_ref[:, :tn] = a_ref[...].astype(BF16)
    o_ref[:, tn:] = b_ref[...].astype(BF16)


def _pair_tiles(a, b, tn):
    k, n = a.shape
    assert a.shape == b.shape and n % tn == 0
    return pl.pallas_call(
        _pair_tiles_kernel,
        grid=(n // tn,),
        in_specs=[pl.BlockSpec((k, tn), lambda j: (0, j))] * 2,
        out_specs=pl.BlockSpec((None, k, 2 * tn), lambda j: (j, 0, 0)),
        out_shape=jax.ShapeDtypeStruct((n // tn, k, 2 * tn), BF16),
        compiler_params=_cparams("parallel"),
        name="wprep_ffn",
    )(a, b)


def _inproj_kernel(x_ref, g_ref, w_ref, wa_ref, *rest):
    if len(rest) == 3:
        side = None
        z_ref, a_ref, xn_ref = rest
    else:
        side = rest[:3] + rest[5:7]
        z_ref, a_ref, xn_ref = rest[3], rest[4], rest[7]

    @pl.when(pl.program_id(1) == 0)
    def _():
        g = g_ref[...]

        def norm(sl):
            xn_ref[sl, :] = _rms(x_ref[sl, :], g).astype(BF16)

        _rows_loop(x_ref.shape[0], norm)
        a_ref[...] = _dot(xn_ref[...], wa_ref[...])

    z_ref[...] = _dot(xn_ref[...], w_ref[...]).astype(z_ref.dtype)

    if side is not None:
        cg_ref, cu_ref, cd_ref, ogu_ref, od_ref = side
        tf = cg_ref.shape[1]
        ogu_ref[:, :tf] = cg_ref[...].astype(BF16)
        ogu_ref[:, tf:] = cu_ref[...].astype(BF16)
        od_ref[...] = cd_ref[...].astype(BF16)


def _side_pieces(steps, nf, d, tf):
    for r in (8, 4, 2, 1):
        if nf * r <= steps and d % (8 * r) == 0 and tf % (16 * r) == 0:
            return r
    return 0


def _inproj(x, g, w, wa, *, tm, ffn_w=None, tf=None):
    m, d = x.shape
    nt, _, tn = w.shape
    n = nt * tn
    tm = _tile(m, tm)
    in_specs = [
        pl.BlockSpec((tm, d), lambda i, j: (i, 0), pipeline_mode=pl.Buffered(1)),
        pl.BlockSpec((1, d), lambda i, j: (0, 0)),
        pl.BlockSpec((None, d, tn), lambda i, j: (j, 0, 0)),
        pl.BlockSpec((d, LANES), lambda i, j: (0, 0)),
    ]
    out_specs = [
        pl.BlockSpec((tm, tn), lambda i, j: (i, j)),
        pl.BlockSpec((tm, LANES), lambda i, j: (i, 0)),
    ]
    out_shape = [jax.ShapeDtypeStruct((m, n), BF16), jax.ShapeDtypeStruct((m, LANES), F32)]
    operands = [x, g, w, wa]
    if ffn_w is not None:
        dff = ffn_w[0].shape[1]
        nf = dff // tf
        r = _side_pieces((m // tm) * nt, nf, d, tf)
        assert r > 0
        last = nf * r - 1

        def piece(i, j):
            return jnp.minimum(i * nt + j, last)

        in_specs += [pl.BlockSpec((d // r, tf), lambda i, j: (piece(i, j) % r, piece(i, j) // r))] * 2
        in_specs += [pl.BlockSpec((tf // r, d), lambda i, j: (piece(i, j), 0))]
        out_specs += [pl.BlockSpec((None, d // r, 2 * tf), lambda i, j: (piece(i, j) // r, piece(i, j) % r, 0)),
                      pl.BlockSpec((tf // r, d), lambda i, j: (piece(i, j), 0))]
        out_shape += [jax.ShapeDtypeStruct((nf, d, 2 * tf), BF16), jax.ShapeDtypeStruct((dff, d), BF16)]
        operands += list(ffn_w)
    return pl.pallas_call(
        _inproj_kernel,
        grid=(m // tm, nt),
        in_specs=in_specs,
        out_specs=out_specs,
        out_shape=out_shape,
        scratch_shapes=[pltpu.VMEM((tm, d), BF16)],
        compiler_params=_cparams("arbitrary", "arbitrary"),
        name="inproj",
    )(*operands)


def _norm_matmul_kernel(x_ref, g_ref, w_ref, o_ref, xn_ref):
    @pl.when(pl.program_id(1) == 0)
    def _():
        g = g_ref[...]

        def norm(sl):
            xn_ref[sl, :] = _rms(x_ref[sl, :], g).astype(BF16)

        _rows_loop(x_ref.shape[0], norm)

    o_ref[...] = _dot(xn_ref[...], w_ref[...])


def _norm_matmul(x, g, w, *, tm, tn):
    m, d = x.shape
    n = w.shape[1]
    tm, tn = _tile(m, tm), _tile(n, tn)
    return pl.pallas_call(
        _norm_matmul_kernel,
        grid=(m // tm, n // tn),
        in_specs=[
            pl.BlockSpec((tm, d), lambda i, j: (i, 0)),
            pl.BlockSpec((1, d), lambda i, j: (0, 0)),
            pl.BlockSpec((d, tn), lambda i, j: (0, j)),
        ],
        out_specs=pl.BlockSpec((tm, tn), lambda i, j: (i, j)),
        out_shape=jax.ShapeDtypeStruct((m, n), F32),
        scratch_shapes=[pltpu.VMEM((tm, d), BF16)],
        compiler_params=_cparams("parallel", "arbitrary"),
        name="memkv",
    )(x, g, w)


def _chunk_cumsum(x, pos, c):
    s = 1
    while s < c:
        x = x + jnp.where(pos >= s, pltpu.roll(x, s, 0), 0.0)
        s *= 2
    return x


def _group_end(x, h):
    n, w = x.shape
    return jnp.concatenate(
        [jnp.broadcast_to(x[(g + 1) * h - 1:(g + 1) * h, :], (h, w)) for g in range(n // h)], axis=0)


def _gla_kernel(q_ref, k_ref, v_ref, r_ref, a_ref, wa2_ref, ba_ref, g_ref, s0_ref,
                og_ref, sout_ref, s_ref, *, chunk, scale, dk, dv):
    t = pl.program_id(2)

    @pl.when(t == 0)
    def _():
        s_ref[...] = s0_ref[0]

    tc = q_ref.shape[0]
    hb = s_ref.shape[0]
    c = chunk
    nch = tc // c
    pos = lax.broadcasted_iota(jnp.int32, (tc, dk), 0) & (c - 1)
    ri = lax.broadcasted_iota(jnp.int32, (c, c), 0)
    ci = lax.broadcasted_iota(jnp.int32, (c, c), 1)
    hs = []
    h = GLA_BLOCK
    while h <= c:
        hs.append(h)
        h *= 2
    blk = GLA_BLOCK.bit_length() - 1
    mask0 = (lax.shift_right_logical(ri, blk) == lax.shift_right_logical(ci, blk)) & (ci <= ri)
    masks = {}
    for h in hs[:-1]:
        sh = (2 * h).bit_length() - 1
        same = lax.shift_right_logical(ri, sh) == lax.shift_right_logical(ci, sh)
        masks[h] = same & ((ri & (2 * h - 1)) >= h) & ((ci & (2 * h - 1)) < h)

    g = g_ref[...]
    a_bf = a_ref[...].astype(BF16)

    for hh in range(hb):
        kcol = slice(hh * dk, (hh + 1) * dk)
        vcol = slice(hh * dv, (hh + 1) * dv)
        la = jax.nn.log_sigmoid(_dot(a_bf, wa2_ref[:, kcol]) + ba_ref[:, kcol]) / GATE_NORM
        beta = _chunk_cumsum(la, pos, c)
        q = q_ref[:, kcol].astype(F32) * scale
        k = k_ref[:, kcol].astype(F32)
        qf, kf = {}, {}
        for h in hs:
            e_h = _group_end(beta, h)
            if h == c:
                s_h = jnp.zeros_like(beta)
                e_c = e_h
            else:
                s_h = jnp.where(pos >= h, pltpu.roll(e_h, h, 0), 0.0)
            qf[h] = (q * jnp.exp(beta - s_h)).astype(BF16)
            kf[h] = (k * jnp.exp(e_h - beta)).astype(BF16)
            if h == GLA_BLOCK:
                k_in = (k * jnp.exp(s_h - beta)).astype(BF16)
        dec_rows = jnp.exp(e_c)

        a_mats, incs, vs = [], [], []
        for i in range(nch):
            sl = slice(i * c, (i + 1) * c)
            a_mat = jnp.where(mask0, _dot_nt(qf[GLA_BLOCK][sl], k_in[sl]), 0.0)
            for h in hs[:-1]:
                a_mat = a_mat + jnp.where(masks[h], _dot_nt(qf[h][sl], kf[h][sl]), 0.0)
            v = v_ref[sl, vcol]
            a_mats.append(a_mat.astype(BF16))
            vs.append(v)
            incs.append(_dot_tn(kf[c][sl], v))

        s_val = s_ref[hh]
        starts = []
        for i in range(nch):
            starts.append(s_val.astype(BF16))
            dec = jnp.transpose(dec_rows[i * c:i * c + 8, :])[:, 0:1]
            s_val = s_val * dec + incs[i]
        s_ref[hh] = s_val

        for i in range(nch):
            sl = slice(i * c, (i + 1) * c)
            o = _dot(a_mats[i], vs[i]) + _dot(qf[c][sl], starts[i])
            on = _rms(o, g)
            r = r_ref[sl, vcol].astype(F32)
            og_ref[sl, vcol] = (on * (r * jax.nn.sigmoid(r))).astype(og_ref.dtype)

    @pl.when(t == pl.num_programs(2) - 1)
    def _():
        sout_ref[0] = s_ref[...]


def _gla(z, a_lr, wa2, ba, g, s0, *, batch, seq, heads, dk, dv, off_q, off_k, off_v, off_r, tc, chunk, hb):
    m = z.shape[0]
    tc = _tile(seq, tc)
    chunk = min(chunk, tc)
    nt = seq // tc
    hb = min(hb, heads)
    wk, wv = hb * dk, hb * dv
    assert heads % hb == 0
    assert off_q % wk == 0 and off_k % wk == 0 and off_v % wv == 0 and off_r % wv == 0
    assert chunk % GLA_BLOCK == 0 and tc % chunk == 0

    def zspec(width, off):
        return pl.BlockSpec((tc, width), lambda b, h, t: (b * nt + t, off // width + h))

    return pl.pallas_call(
        functools.partial(_gla_kernel, chunk=chunk, scale=float(dk) ** -0.5, dk=dk, dv=dv),
        grid=(batch, heads // hb, nt),
        in_specs=[
            zspec(wk, off_q), zspec(wk, off_k), zspec(wv, off_v), zspec(wv, off_r),
            pl.BlockSpec((tc, LANES), lambda b, h, t: (b * nt + t, 0)),
            pl.BlockSpec((LANES, wk), lambda b, h, t: (0, h)),
            pl.BlockSpec((1, wk), lambda b, h, t: (0, h)),
            pl.BlockSpec((1, dv), lambda b, h, t: (0, 0)),
            pl.BlockSpec((1, hb, dk, dv), lambda b, h, t: (b, h, 0, 0)),
        ],
        out_specs=[
            pl.BlockSpec((tc, wv), lambda b, h, t: (b * nt + t, h)),
            pl.BlockSpec((1, hb, dk, dv), lambda b, h, t: (b, h, 0, 0)),
        ],
        out_shape=[jax.ShapeDtypeStruct((m, heads * dv), BF16),
                   jax.ShapeDtypeStruct((batch, heads, dk, dv), F32)],
        scratch_shapes=[pltpu.VMEM((hb, dk, dv), F32)],
        compiler_params=_cparams("parallel", "parallel", "arbitrary"),
        name="gla",
    )(z, z, z, z, a_lr, wa2, ba, g, s0)


def _conv_kernel(cb_ref, cc_ref, ch_ref, buf_ref, w_ref, cg_ref, nb_ref, tail_ref):
    t = pl.program_id(1)

    @pl.when(t == 0)
    def _():
        tail_ref[0:2, :] = buf_ref[0]

    tt = cc_ref.shape[0]
    u = cc_ref[...].astype(F32) * ch_ref[...].astype(F32)
    rows = lax.broadcasted_iota(jnp.int32, u.shape, 0)
    p0 = tail_ref[0:1, :]
    p1 = tail_ref[1:2, :]
    u1 = jnp.where(rows == 0, p1, pltpu.roll(u, 1, 0))
    u2 = jnp.where(rows == 0, p0, jnp.where(rows == 1, p1, pltpu.roll(u, 2, 0)))
    w = w_ref[...]
    conv = u2 * w[0:1, :] + u1 * w[1:2, :] + u * w[2:3, :]
    cg_ref[...] = (cb_ref[...].astype(F32) * conv).astype(cg_ref.dtype)
    last2 = u[tt - 2:tt, :]
    tail_ref[0:2, :] = last2

    @pl.when(t == pl.num_programs(1) - 1)
    def _():
        nb_ref[0] = last2


def _conv(z, buf, w, *, batch, seq, cw, off_cb, tt):
    m = z.shape[0]
    tt = _tile(seq, tt)
    nt = seq // tt
    assert off_cb % cw == 0 and tt >= 2
    kk = buf.shape[1]

    def zspec(k):
        return pl.BlockSpec((tt, cw), lambda b, t: (b * nt + t, off_cb // cw + k))

    return pl.pallas_call(
        _conv_kernel,
        grid=(batch, nt),
        in_specs=[zspec(0), zspec(1), zspec(2),
                  pl.BlockSpec((1, kk, cw), lambda b, t: (b, 0, 0)),
                  pl.BlockSpec((kk + 1, cw), lambda b, t: (0, 0))],
        out_specs=[pl.BlockSpec((tt, cw), lambda b, t: (b * nt + t, 0)),
                   pl.BlockSpec((1, kk, cw), lambda b, t: (b, 0, 0))],
        out_shape=[jax.ShapeDtypeStruct((m, cw), BF16), jax.ShapeDtypeStruct((batch, kk, cw), F32)],
        scratch_shapes=[pltpu.VMEM((8, cw), F32)],
        compiler_params=_cparams("parallel", "arbitrary"),
        name="conv",
    )(z, z, z, buf, w)


def _xattn_kernel(q_ref, *refs, scale, heads, dh):
    if len(refs) == 3:
        ks = pltpu.einshape("mhd->hmd", refs[0][...].astype(BF16))
        vs = pltpu.einshape("mhd->hmd", refs[1][...].astype(BF16))
        k_of, v_of = (lambda h: ks[h]), (lambda h: vs[h])
    else:
        k_of = lambda h: refs[h][...].astype(BF16)
        v_of = lambda h: refs[heads + h][...].astype(BF16)
    o_ref = refs[-1]
    for h in range(heads):
        col = slice(h * dh, (h + 1) * dh)
        s = _dot_nt(q_ref[:, col], k_of(h)) * scale
        p = jnp.exp(s - jnp.max(s, axis=-1, keepdims=True))
        p = p / jnp.sum(p, axis=-1, keepdims=True)
        o_ref[:, col] = _dot(p.astype(BF16), v_of(h)).astype(o_ref.dtype)


def _xattn(z, mem_k, mem_v, *, batch, seq, heads, dh, off_q, voff, tt):
    m = z.shape[0]
    tt = _tile(seq, tt)
    nt = seq // tt
    nm = mem_k.shape[1]
    xw = heads * dh
    assert off_q % xw == 0
    if mem_k.ndim == 4:
        mem_specs = [pl.BlockSpec((None, nm, heads, dh), lambda b, t: (b, 0, 0, 0))] * 2
        mems = (mem_k, mem_v)
    else:
        def spec(h, off):
            return pl.BlockSpec((None, nm, dh), lambda b, t: (b, 0, off * heads + h))
        mem_specs = [spec(h, 0) for h in range(heads)] + [spec(h, voff) for h in range(heads)]
        mems = (mem_k,) * heads + (mem_v,) * heads
    return pl.pallas_call(
        functools.partial(_xattn_kernel, scale=float(dh) ** -0.5, heads=heads, dh=dh),
        grid=(batch, nt),
        in_specs=[pl.BlockSpec((tt, xw), lambda b, t: (b * nt + t, off_q // xw))] + mem_specs,
        out_specs=pl.BlockSpec((tt, xw), lambda b, t: (b * nt + t, 0)),
        out_shape=jax.ShapeDtypeStruct((m, xw), BF16),
        compiler_params=_cparams("parallel", "parallel"),
        name="xattn",
    )(z, *mems)


def _mix_kernel(og_ref, cg_ref, at_ref, m0_ref, m1_ref, m2_ref, bm_ref, wg_ref, wc_ref, wx_ref, o_ref):
    bm = bm_ref[...]
    yg = _dot(og_ref[...], wg_ref[...])
    yc = _dot(cg_ref[...], wc_ref[...])
    yx = _dot(at_ref[...], wx_ref[...])
    mixed = (jax.nn.sigmoid(m0_ref[...].astype(F32) + bm[0:1, :]) * yg
             + jax.nn.sigmoid(m1_ref[...].astype(F32) + bm[1:2, :]) * yc
             + jax.nn.sigmoid(m2_ref[...].astype(F32) + bm[2:3, :]) * yx)
    o_ref[...] = mixed.astype(o_ref.dtype)


def _mix(og, cg, at, z, bm, wg, wc, wx, *, off_m, tm, tn):
    m = og.shape[0]
    d = wg.shape[1]
    tm, tn = _tile(m, tm), _tile(d, tn)
    assert off_m % tn == 0
    nb = bm.shape[0]

    def res(a):
        return pl.BlockSpec((tm, a.shape[1]), lambda i, j: (i, 0))

    def wspec(a):
        return pl.BlockSpec((a.shape[0], tn), lambda i, j: (0, j))

    def mspec(k):
        return pl.BlockSpec((tm, tn), lambda i, j: (i, (off_m + k * d) // tn + j))

    return pl.pallas_call(
        _mix_kernel,
        grid=(m // tm, d // tn),
        in_specs=[res(og), res(cg), res(at), mspec(0), mspec(1), mspec(2),
                  pl.BlockSpec((nb, tn), lambda i, j: (0, j)),
                  wspec(wg), wspec(wc), wspec(wx)],
        out_specs=pl.BlockSpec((tm, tn), lambda i, j: (i, j)),
        out_shape=jax.ShapeDtypeStruct((m, d), BF16),
        compiler_params=_cparams("parallel", "arbitrary"),
        name="mix",
    )(og, cg, at, z, z, z, bm, wg, wc, wx)


def _outproj_kernel(a_ref, w_ref, x_ref, o_ref):
    o_ref[...] = x_ref[...] + _dot(a_ref[...], w_ref[...])


def _outproj(a, w, x, *, tm, tn):
    m, k = a.shape
    n = w.shape[1]
    tm, tn = _tile(m, tm), _tile(n, tn)
    return pl.pallas_call(
        _outproj_kernel,
        grid=(m // tm, n // tn),
        in_specs=[pl.BlockSpec((tm, k), lambda i, j: (i, 0)),
                  pl.BlockSpec((k, tn), lambda i, j: (0, j)),
                  pl.BlockSpec((tm, tn), lambda i, j: (i, j))],
        out_specs=pl.BlockSpec((tm, tn), lambda i, j: (i, j)),
        out_shape=jax.ShapeDtypeStruct((m, n), F32),
        compiler_params=_cparams("parallel", "arbitrary"),
        name="outproj",
    )(a, w, x)


def _ffn_kernel(x_ref, g_ref, wgu_ref, wd_ref, gf_ref, o_ref, hn_ref, *, final_norm):
    f = pl.program_id(1)
    tm = x_ref.shape[0]

    @pl.when(f == 0)
    def _():
        g = g_ref[...]

        def init(sl):
            x = x_ref[sl, :]
            hn_ref[sl, :] = _rms(x, g).astype(BF16)
            o_ref[sl, :] = x

        _rows_loop(tm, init)

    tf = wd_ref.shape[0]
    gu = _dot(hn_ref[...], wgu_ref[...])
    gate, up = gu[:, :tf], gu[:, tf:]
    h = ((gate * jax.nn.sigmoid(gate)) * up).astype(BF16)
    nb = min(FFN_NB, o_ref.shape[1])
    for n in range(o_ref.shape[1] // nb):
        cols = slice(n * nb, (n + 1) * nb)
        o_ref[:, cols] += _dot(h, wd_ref[:, cols])

    if final_norm:
        @pl.when(f == pl.num_programs(1) - 1)
        def _():
            gf = gf_ref[...]

            def fin(sl):
                o_ref[sl, :] = _rms(o_ref[sl, :], gf)

            _rows_loop(tm, fin)


def _ffn(x, g, wgu, wd, gf, *, final_norm, tm):
    m, d = x.shape
    nf, _, tf2 = wgu.shape
    tf = tf2 // 2
    tm = _tile(m, tm)
    return pl.pallas_call(
        functools.partial(_ffn_kernel, final_norm=final_norm),
        grid=(m // tm, nf),
        in_specs=[pl.BlockSpec((tm, d), lambda i, f: (i, 0), pipeline_mode=pl.Buffered(1)),
                  pl.BlockSpec((1, d), lambda i, f: (0, 0)),
                  pl.BlockSpec((None, d, tf2), lambda i, f: (f, 0, 0)),
                  pl.BlockSpec((tf, d), lambda i, f: (f, 0)),
                  pl.BlockSpec((1, d), lambda i, f: (0, 0))],
        out_specs=pl.BlockSpec((tm, d), lambda i, f: (i, 0), pipeline_mode=pl.Buffered(1)),
        out_shape=jax.ShapeDtypeStruct((m, d), F32),
        scratch_shapes=[pltpu.VMEM((tm, d), BF16)],
        compiler_params=_cparams("parallel", "arbitrary"),
        name="ffn",
    )(x, g, wgu, wd, gf)


GLA_CHUNK = 128
GLA_BODIES = 4
FFN_TF = 256
FFN_NB = 1024

INPROJ_TM = 1024


def _layer(x, z, a_lr, batch, seq, s0, buf, mem_k, mem_v, voff, p, dims, *, final_norm):
    d, gk, gv, cw, xw, heads, dk, dv, xh, dh = dims
    off_q, off_k, off_v, off_r = 0, gk, 2 * gk, 2 * gk + gv
    off_cb = off_r + gv
    off_xq = off_cb + 3 * cw
    off_m = off_xq + xw
    tc = _tile(seq, GLA_CHUNK * GLA_BODIES)
    hb = max(1, GLA_BODIES // max(1, tc // GLA_CHUNK))
    og, s_new = _gla(z, a_lr, p["w_a2"], p["b_a"], p["g_gla_out"], s0, batch=batch, seq=seq, heads=heads,
                     dk=dk, dv=dv, off_q=off_q, off_k=off_k, off_v=off_v, off_r=off_r,
                     tc=tc, chunk=GLA_CHUNK, hb=hb)
    cg, new_buf = _conv(z, buf, p["w_conv"], batch=batch, seq=seq, cw=cw, off_cb=off_cb, tt=512)
    at = _xattn(z, mem_k, mem_v, batch=batch, seq=seq, heads=xh, dh=dh, off_q=off_xq, voff=voff, tt=1024)
    mixed = _mix(og, cg, at, z, p["b_merge"], p["w_gla_o"], p["w_conv_o"], p["w_xa_o"],
                 off_m=off_m, tm=1024, tn=1024)
    x1 = _outproj(mixed, p["w_out"], x, tm=1024, tn=1024)
    x2 = _ffn(x1, p["g_ffn"], p["w_ffn_gu"], p["w_ffn_down"], p["g_final"], final_norm=final_norm, tm=1024)
    return x2, s_new, new_buf


def kernel(x_prompt, x_sample, state_gla, cache_conv, cache_mem_k, cache_mem_v, mem_prompt, g_mix, w_in, w_a2, b_a, g_gla_out, w_gla_o, w_conv, w_conv_o, w_xa_o, g_mem, w_mem_kv, b_merge, w_out, g_ffn, w_ffn_gate, w_ffn_up, w_ffn_down, g_final):
    depth, bs, heads, dk, dv = state_gla.shape
    bp, tp, d = x_prompt.shape
    _, ts, _ = x_sample.shape
    gk, gv = heads * dk, heads * dv
    cw = cache_conv.shape[-1]
    _, _, n_mem, xh, dh = cache_mem_k.shape
    xw = xh * dh
    rank = w_a2.shape[1]
    dims = (d, gk, gv, cw, xw, heads, dk, dv, xh, dh)
    off_a = 2 * gk + gv
    assert rank <= LANES
    tf = _tile(w_ffn_gate.shape[-1], FFN_TF)

    hp = x_prompt.reshape(bp * tp, d)
    hs = x_sample.reshape(bs * ts, d)
    row = lambda a: a.reshape(1, -1)
    outs = [[] for _ in range(6)]
    for l in range(depth):
        wt = jnp.transpose(w_in[l])
        p = {
            "g_mix": row(g_mix[l]),
            "w_main": _drop_rows_t(wt, off_a, rank, tn=1024, tk=1024),
            "w_alr": _rows_t_pad(wt, off_a, rank, tk=1024),
            "w_a2": jnp.pad(w_a2[l], ((0, LANES - rank), (0, 0))).astype(BF16),
            "b_a": row(b_a[l]),
            "g_gla_out": row(g_gla_out[l]),
            "w_gla_o": w_gla_o[l].astype(BF16),
            "w_conv": w_conv[l],
            "w_conv_o": w_conv_o[l].astype(BF16),
            "w_xa_o": w_xa_o[l].astype(BF16),
            "b_merge": b_merge[l],
            "w_out": w_out[l].astype(BF16),
            "g_ffn": row(g_ffn[l]),
            "g_final": row(g_final),
        }
        last = l == depth - 1
        ffn_w = (w_ffn_gate[l], w_ffn_up[l], w_ffn_down[l])
        nf = ffn_w[0].shape[1] // tf
        steps = (hp.shape[0] // _tile(hp.shape[0], INPROJ_TM)) * p["w_main"].shape[0]
        if _side_pieces(steps, nf, d, tf):
            zp, ap, p["w_ffn_gu"], p["w_ffn_down"] = _inproj(hp, p["g_mix"], p["w_main"], p["w_alr"],
                                                             tm=INPROJ_TM, ffn_w=ffn_w, tf=tf)
        else:
            zp, ap = _inproj(hp, p["g_mix"], p["w_main"], p["w_alr"], tm=INPROJ_TM)
            p["w_ffn_gu"] = _pair_tiles(ffn_w[0], ffn_w[1], tf)
            p["w_ffn_down"] = ffn_w[2].astype(BF16)
        zs, as_ = _inproj(hs, p["g_mix"], p["w_main"], p["w_alr"], tm=INPROJ_TM)
        kv = _norm_matmul(mem_prompt.reshape(bp * n_mem, d), row(g_mem[l]), w_mem_kv[l].astype(BF16),
                          tm=512, tn=512).reshape(bp, n_mem, 2 * xw)
        s0 = jnp.zeros((bp, heads, dk, dv), state_gla.dtype)
        buf0 = jnp.zeros((bp,) + cache_conv.shape[2:], cache_conv.dtype)
        hp, sp, bufp = _layer(hp, zp, ap, bp, tp, s0, buf0, kv, kv, 1, p, dims, final_norm=last)
        hs, ss, bufs = _layer(hs, zs, as_, bs, ts, state_gla[l], cache_conv[l], cache_mem_k[l], cache_mem_v[l],
                              0, p, dims, final_norm=last)
        new = (sp, bufp, kv[..., :xw].reshape(bp, n_mem, xh, dh), kv[..., xw:].reshape(bp, n_mem, xh, dh),
               ss, bufs)
        for acc, val in zip(outs, new):
            acc.append(val)
    return (hp.reshape(bp, tp, d), hs.reshape(bs, ts, d)) + tuple(jnp.stack(o) for o in outs)
```

```python
import functools

import jax
import jax.numpy as jnp
from jax import lax
from jax.experimental import pallas as pl
from jax.experimental.pallas import tpu as pltpu

F32 = jnp.float32
BF16 = jnp.bfloat16

EPS = 1e-6
GLA_BLOCK = 16
GATE_NORM = 16.0
LANES = 128
V7X_VMEM_BYTES = 64 * 1024 * 1024
VMEM_LIMIT = V7X_VMEM_BYTES - 4 * 1024 * 1024


def _cparams(*sem):
    return pltpu.CompilerParams(dimension_semantics=sem, vmem_limit_bytes=VMEM_LIMIT)


def _tile(n, pref):
    if n <= pref:
        return n
    t = pref
    while n % t:
        t //= 2
    return t


def _rms(x, g):
    ms = jnp.mean(x * x, axis=-1, keepdims=True)
    return x * lax.rsqrt(ms + EPS) * g


def _dot(a, b):
    return jnp.dot(a, b, preferred_element_type=F32)


def _dot_nt(a, b):
    return lax.dot_general(a, b, (((1,), (1,)), ((), ())), preferred_element_type=F32)


def _dot_tn(a, b):
    return lax.dot_general(a, b, (((0,), (0,)), ((), ())), preferred_element_type=F32)


ROW_BLOCK = 32
ROW_UNROLL = 4


def _rows_loop(n_rows, load, store):
    rb = min(ROW_BLOCK, n_rows)
    trips = n_rows // rb
    u = ROW_UNROLL if trips % ROW_UNROLL == 0 else 1

    def body(i, c):
        sls = [pl.ds(pl.multiple_of((i * u + k) * rb, rb), rb) for k in range(u)]
        vals = [load(sl) for sl in sls]
        for sl, v in zip(sls, vals):
            store(sl, v)
        return c

    lax.fori_loop(0, trips // u, body, 0)


def _drop_rows_t_kernel(a_ref, b_ref, o_ref, *, n_plain, shift):
    j = pl.program_id(0)

    @pl.when(j < n_plain)
    def _():
        o_ref[...] = a_ref[...].T.astype(BF16)

    @pl.when(j >= n_plain)
    def _():
        rows = jnp.concatenate([a_ref[shift:, :], b_ref[...]], axis=0)
        o_ref[...] = rows.T.astype(BF16)


def _drop_rows_t(wt, start, width, *, tn, tk):
    rows, cols = wt.shape
    n_out = rows - width
    tn, tk = _tile(n_out, tn), _tile(cols, tk)
    assert start % tn == 0 and width % 8 == 0 and tn % width == 0
    return pl.pallas_call(
        functools.partial(_drop_rows_t_kernel, n_plain=start // tn, shift=width),
        grid=(n_out // tn, cols // tk),
        in_specs=[pl.BlockSpec((tn, tk), lambda j, k: (j, k)),
                  pl.BlockSpec((width, tk), lambda j, k: ((j + 1) * (tn // width), k))],
        out_specs=pl.BlockSpec((None, tk, tn), lambda j, k: (j, k, 0)),
        out_shape=jax.ShapeDtypeStruct((n_out // tn, cols, tn), BF16),
        compiler_params=_cparams("parallel", "parallel"),
        name="wprep",
    )(wt, wt)


def _rows_t_pad_kernel(a_ref, o_ref):
    a = a_ref[...]
    full = jnp.concatenate([a, jnp.zeros((LANES - a.shape[0], a.shape[1]), a.dtype)], axis=0)
    o_ref[...] = full.T.astype(BF16)


def _rows_t_pad(wt, start, width, *, tk):
    cols = wt.shape[1]
    tk = _tile(cols, tk)
    assert start % width == 0 and width % 8 == 0 and width <= LANES
    return pl.pallas_call(
        _rows_t_pad_kernel,
        grid=(cols // tk,),
        in_specs=[pl.BlockSpec((width, tk), lambda k: (start // width, k))],
        out_specs=pl.BlockSpec((tk, LANES), lambda k: (k, 0)),
        out_shape=jax.ShapeDtypeStruct((cols, LANES), BF16),
        compiler_params=_cparams("parallel"),
        name="wprep_alr",
    )(wt)


def _pair_tiles_kernel(a_ref, b_ref, o_ref):
    tn = a_ref.shape[1]
    o_ref[:, :tn] = a_ref[...].astype(BF16)
    o_ref[:, tn:] = b_ref[...].astype(BF16)


def _pair_tiles(a, b, tn):
    k, n = a.shape
    assert a.shape == b.shape and n % tn == 0
    return pl.pallas_call(
        _pair_tiles_kernel,
        grid=(n // tn,),
        in_specs=[pl.BlockSpec((k, tn), lambda j: (0, j))] * 2,
        out_specs=pl.BlockSpec((None, k, 2 * tn), lambda j: (j, 0, 0)),
        out_shape=jax.ShapeDtypeStruct((n // tn, k, 2 * tn), BF16),
        compiler_params=_cparams("parallel"),
        name="wprep_ffn",
    )(a, b)


def _inproj_kernel(x_ref, g_ref, w_ref, wa_ref, *rest):
    if len(rest) == 3:
        side = None
        z_ref, a_ref, xn_ref = rest
    else:
        side = rest[:3] + rest[5:7]
        z_ref, a_ref, xn_ref = rest[3], rest[4], rest[7]

    @pl.when(pl.program_id(1) == 0)
    def _():
        g = g_ref[...]

        def norm(sl, x):
            xn_ref[sl, :] = _rms(x, g).astype(BF16)

        _rows_loop(x_ref.shape[0], lambda sl: x_ref[sl, :], norm)
        a_ref[...] = _dot(xn_ref[...], wa_ref[...])

    z_ref[...] = _dot(xn_ref[...], w_ref[...]).astype(z_ref.dtype)

    if side is not None:
        cg_ref, cu_ref, cd_ref, ogu_ref, od_ref = side
        tf = cg_ref.shape[1]
        ogu_ref[:, :tf] = cg_ref[...].astype(BF16)
        ogu_ref[:, tf:] = cu_ref[...].astype(BF16)
        od_ref[...] = cd_ref[...].astype(BF16)


def _side_pieces(steps, nf, d, tf):
    for r in (8, 4, 2, 1):
        if nf * r <= steps and d % (8 * r) == 0 and tf % (16 * r) == 0:
            return r
    return 0


def _inproj(x, g, w, wa, *, tm, ffn_w=None, tf=None):
    m, d = x.shape
    nt, _, tn = w.shape
    n = nt * tn
    tm = _tile(m, tm)
    in_specs = [
        pl.BlockSpec((tm, d), lambda i, j: (i, 0), pipeline_mode=pl.Buffered(1)),
        pl.BlockSpec((1, d), lambda i, j: (0, 0)),
        pl.BlockSpec((None, d, tn), lambda i, j: (j, 0, 0)),
        pl.BlockSpec((d, LANES), lambda i, j: (0, 0)),
    ]
    out_specs = [
        pl.BlockSpec((tm, tn), lambda i, j: (i, j)),
        pl.BlockSpec((tm, LANES), lambda i, j: (i, 0)),
    ]
    out_shape = [jax.ShapeDtypeStruct((m, n), BF16), jax.ShapeDtypeStruct((m, LANES), F32)]
    operands = [x, g, w, wa]
    if ffn_w is not None:
        dff = ffn_w[0].shape[1]
        nf = dff // tf
        r = _side_pieces((m // tm) * nt, nf, d, tf)
        assert r > 0
        last = nf * r - 1

        def piece(i, j):
            return jnp.minimum(i * nt + j, last)

        in_specs += [pl.BlockSpec((d // r, tf), lambda i, j: (piece(i, j) % r, piece(i, j) // r))] * 2
        in_specs += [pl.BlockSpec((tf // r, d), lambda i, j: (piece(i, j), 0))]
        out_specs += [pl.BlockSpec((None, d // r, 2 * tf), lambda i, j: (piece(i, j) // r, piece(i, j) % r, 0)),
                      pl.BlockSpec((tf // r, d), lambda i, j: (piece(i, j), 0))]
        out_shape += [jax.ShapeDtypeStruct((nf, d, 2 * tf), BF16), jax.ShapeDtypeStruct((dff, d), BF16)]
        operands += list(ffn_w)
    return pl.pallas_call(
        _inproj_kernel,
        grid=(m // tm, nt),
        in_specs=in_specs,
        out_specs=out_specs,
        out_shape=out_shape,
        scratch_shapes=[pltpu.VMEM((tm, d), BF16)],
        compiler_params=_cparams("arbitrary", "arbitrary"),
        name="inproj",
    )(*operands)


def _norm_matmul_kernel(x_ref, g_ref, w_ref, o_ref, xn_ref):
    @pl.when(pl.program_id(1) == 0)
    def _():
        g = g_ref[...]

        def norm(sl, x):
            xn_ref[sl, :] = _rms(x, g).astype(BF16)

        _rows_loop(x_ref.shape[0], lambda sl: x_ref[sl, :], norm)

    o_ref[...] = _dot(xn_ref[...], w_ref[...])


def _norm_matmul(x, g, w, *, tm, tn):
    m, d = x.shape
    n = w.shape[1]
    tm, tn = _tile(m, tm), _tile(n, tn)
    return pl.pallas_call(
        _norm_matmul_kernel,
        grid=(m // tm, n // tn),
        in_specs=[
            pl.BlockSpec((tm, d), lambda i, j: (i, 0)),
            pl.BlockSpec((1, d), lambda i, j: (0, 0)),
            pl.BlockSpec((d, tn), lambda i, j: (0, j)),
        ],
        out_specs=pl.BlockSpec((tm, tn), lambda i, j: (i, j)),
        out_shape=jax.ShapeDtypeStruct((m, n), F32),
        scratch_shapes=[pltpu.VMEM((tm, d), BF16)],
        compiler_params=_cparams("parallel", "arbitrary"),
        name="memkv",
    )(x, g, w)


def _chunk_cumsum(x, pos, c):
    s = 1
    while s < c:
        x = x + jnp.where(pos >= s, pltpu.roll(x, s, 0), 0.0)
        s *= 2
    return x


def _group_end(x, h):
    n, w = x.shape
    return jnp.concatenate(
        [jnp.broadcast_to(x[(g + 1) * h - 1:(g + 1) * h, :], (h, w)) for g in range(n // h)], axis=0)


def _gla_kernel(q_ref, k_ref, v_ref, r_ref, a_ref, wa2_ref, ba_ref, g_ref, s0_ref, *rest,
                chunk, scale, dk, dv, side_ranges):
    n_side = len(side_ranges)
    side_in, (og_ref, sout_ref) = rest[:n_side], rest[n_side:n_side + 2]
    side_out, s_ref = rest[n_side + 2:2 * n_side + 2], rest[2 * n_side + 2]
    _gla_body(q_ref, k_ref, v_ref, r_ref, a_ref, wa2_ref, ba_ref, g_ref, s0_ref, og_ref, sout_ref, s_ref,
              chunk=chunk, scale=scale, dk=dk, dv=dv)
    step = (pl.program_id(0) * pl.num_programs(1) + pl.program_id(1)) * pl.num_programs(2) + pl.program_id(2)
    for src, dst, (lo, hi) in zip(side_in, side_out, side_ranges):
        @pl.when((step >= lo) & (step < hi))
        def _(src=src, dst=dst):
            dst[...] = src[...].astype(BF16)


def _gla_body(q_ref, k_ref, v_ref, r_ref, a_ref, wa2_ref, ba_ref, g_ref, s0_ref,
              og_ref, sout_ref, s_ref, *, chunk, scale, dk, dv):
    t = pl.program_id(2)

    @pl.when(t == 0)
    def _():
        s_ref[...] = s0_ref[0]

    tc = q_ref.shape[0]
    hb = s_ref.shape[0]
    c = chunk
    nch = tc // c
    pos = lax.broadcasted_iota(jnp.int32, (tc, dk), 0) & (c - 1)
    ri = lax.broadcasted_iota(jnp.int32, (c, c), 0)
    ci = lax.broadcasted_iota(jnp.int32, (c, c), 1)
    hs = []
    h = GLA_BLOCK
    while h <= c:
        hs.append(h)
        h *= 2
    blk = GLA_BLOCK.bit_length() - 1
    mask0 = (lax.shift_right_logical(ri, blk) == lax.shift_right_logical(ci, blk)) & (ci <= ri)
    masks = {}
    for h in hs[:-1]:
        sh = (2 * h).bit_length() - 1
        same = lax.shift_right_logical(ri, sh) == lax.shift_right_logical(ci, sh)
        masks[h] = same & ((ri & (2 * h - 1)) >= h) & ((ci & (2 * h - 1)) < h)

    g = g_ref[...]
    a_bf = a_ref[...].astype(BF16)

    for hh in range(hb):
        kcol = slice(hh * dk, (hh + 1) * dk)
        vcol = slice(hh * dv, (hh + 1) * dv)
        la = jax.nn.log_sigmoid(_dot(a_bf, wa2_ref[:, kcol]) + ba_ref[:, kcol]) / GATE_NORM
        beta = _chunk_cumsum(la, pos, c)
        q = q_ref[:, kcol].astype(F32) * scale
        k = k_ref[:, kcol].astype(F32)
        qf, kf = {}, {}
        for h in hs:
            e_h = _group_end(beta, h)
            if h == c:
                s_h = jnp.zeros_like(beta)
                e_c = e_h
            else:
                s_h = jnp.where(pos >= h, pltpu.roll(e_h, h, 0), 0.0)
            qf[h] = (q * jnp.exp(beta - s_h)).astype(BF16)
            kf[h] = (k * jnp.exp(e_h - beta)).astype(BF16)
            if h == GLA_BLOCK:
                k_in = (k * jnp.exp(s_h - beta)).astype(BF16)
        dec_rows = jnp.exp(e_c)

        a_mats, incs, vs = [], [], []
        for i in range(nch):
            sl = slice(i * c, (i + 1) * c)
            a_mat = jnp.where(mask0, _dot_nt(qf[GLA_BLOCK][sl], k_in[sl]), 0.0)
            for h in hs[:-1]:
                a_mat = a_mat + jnp.where(masks[h], _dot_nt(qf[h][sl], kf[h][sl]), 0.0)
            v = v_ref[sl, vcol]
            a_mats.append(a_mat.astype(BF16))
            vs.append(v)
            incs.append(_dot_tn(kf[c][sl], v))

        s_val = s_ref[hh]
        starts = []
        for i in range(nch):
            starts.append(s_val.astype(BF16))
            dec = jnp.transpose(dec_rows[i * c:i * c + 8, :])[:, 0:1]
            s_val = s_val * dec + incs[i]
        s_ref[hh] = s_val

        for i in range(nch):
            sl = slice(i * c, (i + 1) * c)
            o = _dot(a_mats[i], vs[i]) + _dot(qf[c][sl], starts[i])
            on = _rms(o, g)
            r = r_ref[sl, vcol].astype(F32)
            og_ref[sl, vcol] = (on * (r * jax.nn.sigmoid(r))).astype(og_ref.dtype)

    @pl.when(t == pl.num_programs(2) - 1)
    def _():
        sout_ref[0] = s_ref[...]


def _side_rows(steps, arrays):
    pr = 16
    while pr <= max(a.shape[0] for a in arrays):
        if all(a.shape[0] % pr == 0 for a in arrays) and sum(a.shape[0] // pr for a in arrays) <= steps:
            return pr
        pr *= 2
    return 0


def _gla(z, a_lr, wa2, ba, g, s0, *, batch, seq, heads, dk, dv, off_q, off_k, off_v, off_r, tc, chunk, hb,
         cast=()):
    m = z.shape[0]
    tc = _tile(seq, tc)
    chunk = min(chunk, tc)
    nt = seq // tc
    hb = min(hb, heads)
    nh = heads // hb
    wk, wv = hb * dk, hb * dv
    assert heads % hb == 0
    assert off_q % wk == 0 and off_k % wk == 0 and off_v % wv == 0 and off_r % wv == 0
    assert chunk % GLA_BLOCK == 0 and tc % chunk == 0

    def zspec(width, off):
        return pl.BlockSpec((tc, width), lambda b, h, t: (b * nt + t, off // width + h))

    side_specs, side_shapes, side_ranges = [], [], []
    if cast:
        pr = _side_rows(batch * nh * nt, cast)
        assert pr > 0
        start = 0
        for a in cast:
            n_k = a.shape[0] // pr

            def idx(b, h, t, start=start, n_k=n_k):
                return (jnp.clip((b * nh + h) * nt + t - start, 0, n_k - 1), 0)

            side_specs.append(pl.BlockSpec((pr, a.shape[1]), idx))
            side_shapes.append(jax.ShapeDtypeStruct(a.shape, BF16))
            side_ranges.append((start, start + n_k))
            start += n_k

    return pl.pallas_call(
        functools.partial(_gla_kernel, chunk=chunk, scale=float(dk) ** -0.5, dk=dk, dv=dv,
                          side_ranges=tuple(side_ranges)),
        grid=(batch, nh, nt),
        in_specs=[
            zspec(wk, off_q), zspec(wk, off_k), zspec(wv, off_v), zspec(wv, off_r),
            pl.BlockSpec((tc, LANES), lambda b, h, t: (b * nt + t, 0)),
            pl.BlockSpec((LANES, wk), lambda b, h, t: (0, h)),
            pl.BlockSpec((1, wk), lambda b, h, t: (0, h)),
            pl.BlockSpec((1, dv), lambda b, h, t: (0, 0)),
            pl.BlockSpec((1, hb, dk, dv), lambda b, h, t: (b, h, 0, 0)),
        ] + side_specs,
        out_specs=[
            pl.BlockSpec((tc, wv), lambda b, h, t: (b * nt + t, h)),
            pl.BlockSpec((1, hb, dk, dv), lambda b, h, t: (b, h, 0, 0)),
        ] + side_specs,
        out_shape=[jax.ShapeDtypeStruct((m, heads * dv), BF16),
                   jax.ShapeDtypeStruct((batch, heads, dk, dv), F32)] + side_shapes,
        scratch_shapes=[pltpu.VMEM((hb, dk, dv), F32)],
        compiler_params=_cparams("arbitrary", "arbitrary", "arbitrary"),
        name="gla",
    )(z, z, z, z, a_lr, wa2, ba, g, s0, *cast)


def _conv_kernel(cb_ref, cc_ref, ch_ref, buf_ref, w_ref, cg_ref, nb_ref, tail_ref):
    t = pl.program_id(1)

    @pl.when(t == 0)
    def _():
        tail_ref[0:2, :] = buf_ref[0]

    tt = cc_ref.shape[0]
    u = cc_ref[...].astype(F32) * ch_ref[...].astype(F32)
    rows = lax.broadcasted_iota(jnp.int32, u.shape, 0)
    p0 = tail_ref[0:1, :]
    p1 = tail_ref[1:2, :]
    u1 = jnp.where(rows == 0, p1, pltpu.roll(u, 1, 0))
    u2 = jnp.where(rows == 0, p0, jnp.where(rows == 1, p1, pltpu.roll(u, 2, 0)))
    w = w_ref[...]
    conv = u2 * w[0:1, :] + u1 * w[1:2, :] + u * w[2:3, :]
    cg_ref[...] = (cb_ref[...].astype(F32) * conv).astype(cg_ref.dtype)
    last2 = u[tt - 2:tt, :]
    tail_ref[0:2, :] = last2

    @pl.when(t == pl.num_programs(1) - 1)
    def _():
        nb_ref[0] = last2


def _conv(z, buf, w, *, batch, seq, cw, off_cb, tt):
    m = z.shape[0]
    tt = _tile(seq, tt)
    nt = seq // tt
    assert off_cb % cw == 0 and tt >= 2
    kk = buf.shape[1]

    def zspec(k):
        return pl.BlockSpec((tt, cw), lambda b, t: (b * nt + t, off_cb // cw + k))

    return pl.pallas_call(
        _conv_kernel,
        grid=(batch, nt),
        in_specs=[zspec(0), zspec(1), zspec(2),
                  pl.BlockSpec((1, kk, cw), lambda b, t: (b, 0, 0)),
                  pl.BlockSpec((kk + 1, cw), lambda b, t: (0, 0))],
        out_specs=[pl.BlockSpec((tt, cw), lambda b, t: (b * nt + t, 0)),
                   pl.BlockSpec((1, kk, cw), lambda b, t: (b, 0, 0))],
        out_shape=[jax.ShapeDtypeStruct((m, cw), BF16), jax.ShapeDtypeStruct((batch, kk, cw), F32)],
        scratch_shapes=[pltpu.VMEM((8, cw), F32)],
        compiler_params=_cparams("parallel", "arbitrary"),
        name="conv",
    )(z, z, z, buf, w)


def _xattn_kernel(q_ref, *refs, scale, heads, dh):
    if len(refs) == 3:
        ks = pltpu.einshape("mhd->hmd", refs[0][...].astype(BF16))
        vs = pltpu.einshape("mhd->hmd", refs[1][...].astype(BF16))
        k_of, v_of = (lambda h: ks[h]), (lambda h: vs[h])
    else:
        k_of = lambda h: refs[h][...].astype(BF16)
        v_of = lambda h: refs[heads + h][...].astype(BF16)
    o_ref = refs[-1]
    for h in range(heads):
        col = slice(h * dh, (h + 1) * dh)
        s = _dot_nt(q_ref[:, col], k_of(h)) * scale
        p = jnp.exp(s - jnp.max(s, axis=-1, keepdims=True))
        p = p / jnp.sum(p, axis=-1, keepdims=True)
        o_ref[:, col] = _dot(p.astype(BF16), v_of(h)).astype(o_ref.dtype)


def _xattn(z, mem_k, mem_v, *, batch, seq, heads, dh, off_q, voff, tt):
    m = z.shape[0]
    tt = _tile(seq, tt)
    nt = seq // tt
    nm = mem_k.shape[1]
    xw = heads * dh
    assert off_q % xw == 0
    if mem_k.ndim == 4:
        mem_specs = [pl.BlockSpec((None, nm, heads, dh), lambda b, t: (b, 0, 0, 0))] * 2
        mems = (mem_k, mem_v)
    else:
        def spec(h, off):
            return pl.BlockSpec((None, nm, dh), lambda b, t: (b, 0, off * heads + h))
        mem_specs = [spec(h, 0) for h in range(heads)] + [spec(h, voff) for h in range(heads)]
        mems = (mem_k,) * heads + (mem_v,) * heads
    return pl.pallas_call(
        functools.partial(_xattn_kernel, scale=float(dh) ** -0.5, heads=heads, dh=dh),
        grid=(batch, nt),
        in_specs=[pl.BlockSpec((tt, xw), lambda b, t: (b * nt + t, off_q // xw))] + mem_specs,
        out_specs=pl.BlockSpec((tt, xw), lambda b, t: (b * nt + t, 0)),
        out_shape=jax.ShapeDtypeStruct((m, xw), BF16),
        compiler_params=_cparams("parallel", "parallel"),
        name="xattn",
    )(z, *mems)


def _mix_kernel(og_ref, cg_ref, at_ref, m0_ref, m1_ref, m2_ref, bm_ref, wg_ref, wc_ref, wx_ref, o_ref):
    bm = bm_ref[...]
    yg = _dot(og_ref[...], wg_ref[...])
    yc = _dot(cg_ref[...], wc_ref[...])
    yx = _dot(at_ref[...], wx_ref[...])
    mixed = (jax.nn.sigmoid(m0_ref[...].astype(F32) + bm[0:1, :]) * yg
             + jax.nn.sigmoid(m1_ref[...].astype(F32) + bm[1:2, :]) * yc
             + jax.nn.sigmoid(m2_ref[...].astype(F32) + bm[2:3, :]) * yx)
    o_ref[...] = mixed.astype(o_ref.dtype)


def _mix(og, cg, at, z, bm, wg, wc, wx, *, off_m, tm, tn):
    m = og.shape[0]
    d = wg.shape[1]
    tm, tn = _tile(m, tm), _tile(d, tn)
    assert off_m % tn == 0
    nb = bm.shape[0]

    def res(a):
        return pl.BlockSpec((tm, a.shape[1]), lambda i, j: (i, 0))

    def wspec(a):
        return pl.BlockSpec((a.shape[0], tn), lambda i, j: (0, j))

    def mspec(k):
        return pl.BlockSpec((tm, tn), lambda i, j: (i, (off_m + k * d) // tn + j))

    return pl.pallas_call(
        _mix_kernel,
        grid=(m // tm, d // tn),
        in_specs=[res(og), res(cg), res(at), mspec(0), mspec(1), mspec(2),
                  pl.BlockSpec((nb, tn), lambda i, j: (0, j)),
                  wspec(wg), wspec(wc), wspec(wx)],
        out_specs=pl.BlockSpec((tm, tn), lambda i, j: (i, j)),
        out_shape=jax.ShapeDtypeStruct((m, d), BF16),
        compiler_params=_cparams("parallel", "arbitrary"),
        name="mix",
    )(og, cg, at, z, z, z, bm, wg, wc, wx)


def _outproj_kernel(a_ref, w_ref, x_ref, o_ref):
    o_ref[...] = x_ref[...] + _dot(a_ref[...], w_ref[...])


def _outproj(a, w, x, *, tm, tn):
    m, k = a.shape
    n = w.shape[1]
    tm, tn = _tile(m, tm), _tile(n, tn)
    return pl.pallas_call(
        _outproj_kernel,
        grid=(m // tm, n // tn),
        in_specs=[pl.BlockSpec((tm, k), lambda i, j: (i, 0)),
                  pl.BlockSpec((k, tn), lambda i, j: (0, j)),
                  pl.BlockSpec((tm, tn), lambda i, j: (i, j))],
        out_specs=pl.BlockSpec((tm, tn), lambda i, j: (i, j)),
        out_shape=jax.ShapeDtypeStruct((m, n), F32),
        compiler_params=_cparams("parallel", "arbitrary"),
        name="outproj",
    )(a, w, x)


def _ffn_kernel(x_ref, g_ref, wgu_ref, wd_ref, gf_ref, o_ref, hn_ref, *, final_norm):
    f = pl.program_id(1)
    tm = x_ref.shape[0]

    @pl.when(f == 0)
    def _():
        g = g_ref[...]

        def init(sl, x):
            hn_ref[sl, :] = _rms(x, g).astype(BF16)
            o_ref[sl, :] = x

        _rows_loop(tm, lambda sl: x_ref[sl, :], init)

    tf = wd_ref.shape[0]
    gu = _dot(hn_ref[...], wgu_ref[...])
    gate, up = gu[:, :tf], gu[:, tf:]
    h = ((gate * jax.nn.sigmoid(gate)) * up).astype(BF16)
    nb = min(FFN_NB, o_ref.shape[1])
    for n in range(o_ref.shape[1] // nb):
        cols = slice(n * nb, (n + 1) * nb)
        o_ref[:, cols] += _dot(h, wd_ref[:, cols])

    if final_norm:
        @pl.when(f == pl.num_programs(1) - 1)
        def _():
            gf = gf_ref[...]

            def fin(sl, acc):
                o_ref[sl, :] = _rms(acc, gf)

            _rows_loop(tm, lambda sl: o_ref[sl, :], fin)


def _ffn(x, g, wgu, wd, gf, *, final_norm, tm):
    m, d = x.shape
    nf, _, tf2 = wgu.shape
    tf = tf2 // 2
    tm = _tile(m, tm)
    return pl.pallas_call(
        functools.partial(_ffn_kernel, final_norm=final_norm),
        grid=(m // tm, nf),
        in_specs=[pl.BlockSpec((tm, d), lambda i, f: (i, 0), pipeline_mode=pl.Buffered(1)),
                  pl.BlockSpec((1, d), lambda i, f: (0, 0)),
                  pl.BlockSpec((None, d, tf2), lambda i, f: (f, 0, 0)),
                  pl.BlockSpec((tf, d), lambda i, f: (f, 0)),
                  pl.BlockSpec((1, d), lambda i, f: (0, 0))],
        out_specs=pl.BlockSpec((tm, d), lambda i, f: (i, 0), pipeline_mode=pl.Buffered(1)),
        out_shape=jax.ShapeDtypeStruct((m, d), F32),
        scratch_shapes=[pltpu.VMEM((tm, d), BF16)],
        compiler_params=_cparams("parallel", "arbitrary"),
        name="ffn",
    )(x, g, wgu, wd, gf)


GLA_CHUNK = 128
GLA_BODIES = 4
FFN_TF = 256
FFN_NB = 1024

INPROJ_TM = 1024
PROJ_WEIGHTS = ("w_gla_o", "w_conv_o", "w_xa_o", "w_out")


def _layer(x, z, a_lr, batch, seq, s0, buf, mem_k, mem_v, voff, p, dims, *, final_norm):
    d, gk, gv, cw, xw, heads, dk, dv, xh, dh = dims
    off_q, off_k, off_v, off_r = 0, gk, 2 * gk, 2 * gk + gv
    off_cb = off_r + gv
    off_xq = off_cb + 3 * cw
    off_m = off_xq + xw
    tc = _tile(seq, GLA_CHUNK * GLA_BODIES)
    hb = max(1, GLA_BODIES // max(1, tc // GLA_CHUNK))
    names = [k for k in PROJ_WEIGHTS if p[k].dtype != BF16]
    steps = batch * (heads // min(hb, heads)) * (seq // tc)
    if names and not _side_rows(steps, [p[k] for k in names]):
        for k in names:
            p[k] = p[k].astype(BF16)
        names = []
    og, s_new, *cast = _gla(z, a_lr, p["w_a2"], p["b_a"], p["g_gla_out"], s0, batch=batch, seq=seq, heads=heads,
                            dk=dk, dv=dv, off_q=off_q, off_k=off_k, off_v=off_v, off_r=off_r,
                            tc=tc, chunk=GLA_CHUNK, hb=hb, cast=tuple(p[k] for k in names))
    p.update(zip(names, cast))
    cg, new_buf = _conv(z, buf, p["w_conv"], batch=batch, seq=seq, cw=cw, off_cb=off_cb, tt=512)
    at = _xattn(z, mem_k, mem_v, batch=batch, seq=seq, heads=xh, dh=dh, off_q=off_xq, voff=voff, tt=1024)
    mixed = _mix(og, cg, at, z, p["b_merge"], p["w_gla_o"], p["w_conv_o"], p["w_xa_o"],
                 off_m=off_m, tm=1024, tn=1024)
    x1 = _outproj(mixed, p["w_out"], x, tm=1024, tn=1024)
    x2 = _ffn(x1, p["g_ffn"], p["w_ffn_gu"], p["w_ffn_down"], p["g_final"], final_norm=final_norm, tm=1024)
    return x2, s_new, new_buf


def kernel(x_prompt, x_sample, state_gla, cache_conv, cache_mem_k, cache_mem_v, mem_prompt, g_mix, w_in, w_a2, b_a, g_gla_out, w_gla_o, w_conv, w_conv_o, w_xa_o, g_mem, w_mem_kv, b_merge, w_out, g_ffn, w_ffn_gate, w_ffn_up, w_ffn_down, g_final):
    depth, bs, heads, dk, dv = state_gla.shape
    bp, tp, d = x_prompt.shape
    _, ts, _ = x_sample.shape
    gk, gv = heads * dk, heads * dv
    cw = cache_conv.shape[-1]
    _, _, n_mem, xh, dh = cache_mem_k.shape
    xw = xh * dh
    rank = w_a2.shape[1]
    dims = (d, gk, gv, cw, xw, heads, dk, dv, xh, dh)
    off_a = 2 * gk + gv
    assert rank <= LANES
    tf = _tile(w_ffn_gate.shape[-1], FFN_TF)

    hp = x_prompt.reshape(bp * tp, d)
    hs = x_sample.reshape(bs * ts, d)
    row = lambda a: a.reshape(1, -1)
    outs = [[] for _ in range(6)]
    for l in range(depth):
        wt = jnp.transpose(w_in[l])
        p = {
            "g_mix": row(g_mix[l]),
            "w_main": _drop_rows_t(wt, off_a, rank, tn=1024, tk=1024),
            "w_alr": _rows_t_pad(wt, off_a, rank, tk=1024),
            "w_a2": jnp.pad(w_a2[l], ((0, LANES - rank), (0, 0))).astype(BF16),
            "b_a": row(b_a[l]),
            "g_gla_out": row(g_gla_out[l]),
            "w_gla_o": w_gla_o[l],
            "w_conv": w_conv[l],
            "w_conv_o": w_conv_o[l],
            "w_xa_o": w_xa_o[l],
            "b_merge": b_merge[l],
            "w_out": w_out[l],
            "g_ffn": row(g_ffn[l]),
            "g_final": row(g_final),
        }
        last = l == depth - 1
        ffn_w = (w_ffn_gate[l], w_ffn_up[l], w_ffn_down[l])
        nf = ffn_w[0].shape[1] // tf
        steps = (hp.shape[0] // _tile(hp.shape[0], INPROJ_TM)) * p["w_main"].shape[0]
        if _side_pieces(steps, nf, d, tf):
            zp, ap, p["w_ffn_gu"], p["w_ffn_down"] = _inproj(hp, p["g_mix"], p["w_main"], p["w_alr"],
                                                             tm=INPROJ_TM, ffn_w=ffn_w, tf=tf)
        else:
            zp, ap = _inproj(hp, p["g_mix"], p["w_main"], p["w_alr"], tm=INPROJ_TM)
            p["w_ffn_gu"] = _pair_tiles(ffn_w[0], ffn_w[1], tf)
            p["w_ffn_down"] = ffn_w[2].astype(BF16)
        zs, as_ = _inproj(hs, p["g_mix"], p["w_main"], p["w_alr"], tm=INPROJ_TM)
        kv = _norm_matmul(mem_prompt.reshape(bp * n_mem, d), row(g_mem[l]), w_mem_kv[l].astype(BF16),
                          tm=512, tn=512).reshape(bp, n_mem, 2 * xw)
        s0 = jnp.zeros((bp, heads, dk, dv), state_gla.dtype)
        buf0 = jnp.zeros((bp,) + cache_conv.shape[2:], cache_conv.dtype)
        hp, sp, bufp = _layer(hp, zp, ap, bp, tp, s0, buf0, kv, kv, 1, p, dims, final_norm=last)
        hs, ss, bufs = _layer(hs, zs, as_, bs, ts, state_gla[l], cache_conv[l], cache_mem_k[l], cache_mem_v[l],
                              0, p, dims, final_norm=last)
        new = (sp, bufp, kv[..., :xw].reshape(bp, n_mem, xh, dh), kv[..., xw:].reshape(bp, n_mem, xh, dh),
               ss, bufs)
        for acc, val in zip(outs, new):
            acc.append(val)
    return (hp.reshape(bp, tp, d), hs.reshape(bs, ts, d)) + tuple(jnp.stack(o) for o in outs)
```

```python
import functools

import jax
import jax.numpy as jnp
from jax import lax
from jax.experimental import pallas as pl
from jax.experimental.pallas import tpu as pltpu

F32 = jnp.float32
BF16 = jnp.bfloat16

EPS = 1e-6
GLA_BLOCK = 16
GATE_NORM = 16.0
LANES = 128
V7X_VMEM_BYTES = 64 * 1024 * 1024
VMEM_LIMIT = V7X_VMEM_BYTES - 4 * 1024 * 1024


def _cparams(*sem):
    return pltpu.CompilerParams(dimension_semantics=sem, vmem_limit_bytes=VMEM_LIMIT)


def _tile(n, pref):
    if n <= pref:
        return n
    t = pref
    while n % t:
        t //= 2
    return t


def _rms(x, g):
    ms = jnp.mean(x * x, axis=-1, keepdims=True)
    return x * lax.rsqrt(ms + EPS) * g


def _dot(a, b):
    return jnp.dot(a, b, preferred_element_type=F32)


def _dot_nt(a, b):
    return lax.dot_general(a, b, (((1,), (1,)), ((), ())), preferred_element_type=F32)


def _dot_tn(a, b):
    return lax.dot_general(a, b, (((0,), (0,)), ((), ())), preferred_element_type=F32)


ROW_BLOCK = 16
ROW_UNROLL = 8


def _rows_loop(n_rows, load, store):
    rb = min(ROW_BLOCK, n_rows)
    trips = n_rows // rb
    u = ROW_UNROLL if trips % ROW_UNROLL == 0 else 1

    def body(i, c):
        sls = [pl.ds(pl.multiple_of((i * u + k) * rb, rb), rb) for k in range(u)]
        vals = [load(sl) for sl in sls]
        for sl, v in zip(sls, vals):
            store(sl, v)
        return c

    lax.fori_loop(0, trips // u, body, 0)


def _rms_rows(n_rows, load, emit, g):
    _rows_loop(n_rows, load, lambda sl, x: emit(sl, x, _rms(x, g)))


def _drop_rows_t_kernel(a_ref, b_ref, o_ref, *, n_plain, shift):
    j = pl.program_id(0)

    @pl.when(j < n_plain)
    def _():
        o_ref[...] = a_ref[...].T.astype(BF16)

    @pl.when(j >= n_plain)
    def _():
        rows = jnp.concatenate([a_ref[shift:, :], b_ref[...]], axis=0)
        o_ref[...] = rows.T.astype(BF16)


def _drop_rows_t(wt, start, width, *, tn, tk):
    rows, cols = wt.shape
    n_out = rows - width
    tn, tk = _tile(n_out, tn), _tile(cols, tk)
    assert start % tn == 0 and width % 8 == 0 and tn % width == 0
    return pl.pallas_call(
        functools.partial(_drop_rows_t_kernel, n_plain=start // tn, shift=width),
        grid=(n_out // tn, cols // tk),
        in_specs=[pl.BlockSpec((tn, tk), lambda j, k: (j, k)),
                  pl.BlockSpec((width, tk), lambda j, k: ((j + 1) * (tn // width), k))],
        out_specs=pl.BlockSpec((None, tk, tn), lambda j, k: (j, k, 0)),
        out_shape=jax.ShapeDtypeStruct((n_out // tn, cols, tn), BF16),
        compiler_params=_cparams("parallel", "parallel"),
        name="wprep",
    )(wt, wt)


def _rows_t_pad_kernel(a_ref, o_ref):
    a = a_ref[...]
    full = jnp.concatenate([a, jnp.zeros((LANES - a.shape[0], a.shape[1]), a.dtype)], axis=0)
    o_ref[...] = full.T.astype(BF16)


def _rows_t_pad(wt, start, width, *, tk):
    cols = wt.shape[1]
    tk = _tile(cols, tk)
    assert start % width == 0 and width % 8 == 0 and width <= LANES
    return pl.pallas_call(
        _rows_t_pad_kernel,
        grid=(cols // tk,),
        in_specs=[pl.BlockSpec((width, tk), lambda k: (start // width, k))],
        out_specs=pl.BlockSpec((tk, LANES), lambda k: (k, 0)),
        out_shape=jax.ShapeDtypeStruct((cols, LANES), BF16),
        compiler_params=_cparams("parallel"),
        name="wprep_alr",
    )(wt)


def _pair_tiles_kernel(a_ref, b_ref, o_ref):
    tn = a_ref.shape[1]
    o_ref[:, :tn] = a_ref[...].astype(BF16)
    o_ref[:, tn:] = b_ref[...].astype(BF16)


def _pair_tiles(a, b, tn):
    k, n = a.shape
    assert a.shape == b.shape and n % tn == 0
    return pl.pallas_call(
        _pair_tiles_kernel,
        grid=(n // tn,),
        in_specs=[pl.BlockSpec((k, tn), lambda j: (0, j))] * 2,
        out_specs=pl.BlockSpec((None, k, 2 * tn), lambda j: (j, 0, 0)),
        out_shape=jax.ShapeDtypeStruct((n // tn, k, 2 * tn), BF16),
        compiler_params=_cparams("parallel"),
        name="wprep_ffn",
    )(a, b)


def _inproj_kernel(x_ref, g_ref, w_ref, wa_ref, *rest):
    if len(rest) == 3:
        side = None
        z_ref, a_ref, xn_ref = rest
    else:
        side = rest[:3] + rest[5:7]
        z_ref, a_ref, xn_ref = rest[3], rest[4], rest[7]

    @pl.when(pl.program_id(1) == 0)
    def _():
        def emit(sl, x, xn):
            xn_ref[sl, :] = xn.astype(BF16)

        _rms_rows(x_ref.shape[0], lambda sl: x_ref[sl, :], emit, g_ref[...])
        a_ref[...] = _dot(xn_ref[...], wa_ref[...])

    z_ref[...] = _dot(xn_ref[...], w_ref[...]).astype(z_ref.dtype)

    if side is not None:
        cg_ref, cu_ref, cd_ref, ogu_ref, od_ref = side
        tf = cg_ref.shape[1]
        ogu_ref[:, :tf] = cg_ref[...].astype(BF16)
        ogu_ref[:, tf:] = cu_ref[...].astype(BF16)
        od_ref[...] = cd_ref[...].astype(BF16)


def _side_pieces(steps, nf, d, tf):
    for r in (8, 4, 2, 1):
        if nf * r <= steps and d % (8 * r) == 0 and tf % (16 * r) == 0:
            return r
    return 0


def _inproj(x, g, w, wa, *, tm, ffn_w=None, tf=None):
    m, d = x.shape
    nt, _, tn = w.shape
    n = nt * tn
    tm = _tile(m, tm)
    in_specs = [
        pl.BlockSpec((tm, d), lambda i, j: (i, 0), pipeline_mode=pl.Buffered(1)),
        pl.BlockSpec((1, d), lambda i, j: (0, 0)),
        pl.BlockSpec((None, d, tn), lambda i, j: (j, 0, 0)),
        pl.BlockSpec((d, LANES), lambda i, j: (0, 0)),
    ]
    out_specs = [
        pl.BlockSpec((tm, tn), lambda i, j: (i, j)),
        pl.BlockSpec((tm, LANES), lambda i, j: (i, 0)),
    ]
    out_shape = [jax.ShapeDtypeStruct((m, n), BF16), jax.ShapeDtypeStruct((m, LANES), F32)]
    operands = [x, g, w, wa]
    if ffn_w is not None:
        dff = ffn_w[0].shape[1]
        nf = dff // tf
        r = _side_pieces((m // tm) * nt, nf, d, tf)
        assert r > 0
        last = nf * r - 1

        def piece(i, j):
            return jnp.minimum(i * nt + j, last)

        in_specs += [pl.BlockSpec((d // r, tf), lambda i, j: (piece(i, j) % r, piece(i, j) // r))] * 2
        in_specs += [pl.BlockSpec((tf // r, d), lambda i, j: (piece(i, j), 0))]
        out_specs += [pl.BlockSpec((None, d // r, 2 * tf), lambda i, j: (piece(i, j) // r, piece(i, j) % r, 0)),
                      pl.BlockSpec((tf // r, d), lambda i, j: (piece(i, j), 0))]
        out_shape += [jax.ShapeDtypeStruct((nf, d, 2 * tf), BF16), jax.ShapeDtypeStruct((dff, d), BF16)]
        operands += list(ffn_w)
    return pl.pallas_call(
        _inproj_kernel,
        grid=(m // tm, nt),
        in_specs=in_specs,
        out_specs=out_specs,
        out_shape=out_shape,
        scratch_shapes=[pltpu.VMEM((tm, d), BF16)],
        compiler_params=_cparams("arbitrary", "arbitrary"),
        name="inproj",
    )(*operands)


def _norm_matmul_kernel(x_ref, g_ref, w_ref, o_ref, xn_ref):
    @pl.when(pl.program_id(1) == 0)
    def _():
        def emit(sl, x, xn):
            xn_ref[sl, :] = xn.astype(BF16)

        _rms_rows(x_ref.shape[0], lambda sl: x_ref[sl, :], emit, g_ref[...])

    o_ref[...] = _dot(xn_ref[...], w_ref[...])


def _norm_matmul(x, g, w, *, tm, tn):
    m, d = x.shape
    n = w.shape[1]
    tm, tn = _tile(m, tm), _tile(n, tn)
    return pl.pallas_call(
        _norm_matmul_kernel,
        grid=(m // tm, n // tn),
        in_specs=[
            pl.BlockSpec((tm, d), lambda i, j: (i, 0)),
            pl.BlockSpec((1, d), lambda i, j: (0, 0)),
            pl.BlockSpec((d, tn), lambda i, j: (0, j)),
        ],
        out_specs=pl.BlockSpec((tm, tn), lambda i, j: (i, j)),
        out_shape=jax.ShapeDtypeStruct((m, n), F32),
        scratch_shapes=[pltpu.VMEM((tm, d), BF16)],
        compiler_params=_cparams("parallel", "arbitrary"),
        name="memkv",
    )(x, g, w)


def _log_sigmoid(x):
    return jnp.minimum(x, 0.0) - jnp.log(1.0 + jnp.exp(-jnp.abs(x)))


def _chunk_cumsum(x, pos, c):
    s = 1
    while s < c:
        x = x + jnp.where(pos >= s, pltpu.roll(x, s, 0), 0.0)
        s *= 2
    return x


def _group_end(x, h):
    n, w = x.shape
    return jnp.concatenate(
        [jnp.broadcast_to(x[(g + 1) * h - 1:(g + 1) * h, :], (h, w)) for g in range(n // h)], axis=0)


def _gla_kernel(q_ref, k_ref, v_ref, r_ref, a_ref, wa2_ref, ba_ref, g_ref, s0_ref, *rest,
                chunk, scale, dk, dv, side_ranges):
    n_side = len(side_ranges)
    side_in, (og_ref, sout_ref) = rest[:n_side], rest[n_side:n_side + 2]
    side_out, s_ref = rest[n_side + 2:2 * n_side + 2], rest[2 * n_side + 2]
    _gla_body(q_ref, k_ref, v_ref, r_ref, a_ref, wa2_ref, ba_ref, g_ref, s0_ref, og_ref, sout_ref, s_ref,
              chunk=chunk, scale=scale, dk=dk, dv=dv)
    step = (pl.program_id(0) * pl.num_programs(1) + pl.program_id(1)) * pl.num_programs(2) + pl.program_id(2)
    for src, dst, (lo, hi) in zip(side_in, side_out, side_ranges):
        @pl.when((step >= lo) & (step < hi))
        def _(src=src, dst=dst):
            dst[...] = src[...].astype(BF16)


def _gla_body(q_ref, k_ref, v_ref, r_ref, a_ref, wa2_ref, ba_ref, g_ref, s0_ref,
              og_ref, sout_ref, s_ref, *, chunk, scale, dk, dv):
    t = pl.program_id(2)

    @pl.when(t == 0)
    def _():
        s_ref[...] = s0_ref[0]

    tc = q_ref.shape[0]
    hb = s_ref.shape[0]
    c = chunk
    nch = tc // c
    pos = lax.broadcasted_iota(jnp.int32, (tc, dk), 0) & (c - 1)
    ri = lax.broadcasted_iota(jnp.int32, (c, c), 0)
    ci = lax.broadcasted_iota(jnp.int32, (c, c), 1)
    hs = []
    h = GLA_BLOCK
    while h <= c:
        hs.append(h)
        h *= 2
    blk = GLA_BLOCK.bit_length() - 1
    mask0 = (lax.shift_right_logical(ri, blk) == lax.shift_right_logical(ci, blk)) & (ci <= ri)
    masks = {}
    for h in hs[:-1]:
        sh = (2 * h).bit_length() - 1
        same = lax.shift_right_logical(ri, sh) == lax.shift_right_logical(ci, sh)
        masks[h] = same & ((ri & (2 * h - 1)) >= h) & ((ci & (2 * h - 1)) < h)

    g = g_ref[...]
    a_bf = a_ref[...].astype(BF16)

    for hh in range(hb):
        kcol = slice(hh * dk, (hh + 1) * dk)
        vcol = slice(hh * dv, (hh + 1) * dv)
        la = _log_sigmoid(_dot(a_bf, wa2_ref[:, kcol]) + ba_ref[:, kcol]) / GATE_NORM
        beta = _chunk_cumsum(la, pos, c)
        q = q_ref[:, kcol].astype(F32) * scale
        k = k_ref[:, kcol].astype(F32)
        qf, kf = {}, {}
        for h in hs:
            e_h = _group_end(beta, h)
            if h == c:
                s_h = jnp.zeros_like(beta)
                e_c = e_h
            else:
                s_h = jnp.where(pos >= h, pltpu.roll(e_h, h, 0), 0.0)
            qf[h] = (q * jnp.exp(beta - s_h)).astype(BF16)
            kf[h] = (k * jnp.exp(e_h - beta)).astype(BF16)
            if h == GLA_BLOCK:
                k_in = (k * jnp.exp(s_h - beta)).astype(BF16)
        dec_rows = jnp.exp(e_c)

        a_mats, incs, vs = [], [], []
        for i in range(nch):
            sl = slice(i * c, (i + 1) * c)
            a_mat = jnp.where(mask0, _dot_nt(qf[GLA_BLOCK][sl], k_in[sl]), 0.0)
            for h in hs[:-1]:
                a_mat = a_mat + jnp.where(masks[h], _dot_nt(qf[h][sl], kf[h][sl]), 0.0)
            v = v_ref[sl, vcol]
            a_mats.append(a_mat.astype(BF16))
            vs.append(v)
            incs.append(_dot_tn(kf[c][sl], v))

        s_val = s_ref[hh]
        starts = []
        for i in range(nch):
            starts.append(s_val.astype(BF16))
            dec = jnp.transpose(dec_rows[i * c:i * c + 8, :])[:, 0:1]
            s_val = s_val * dec + incs[i]
        s_ref[hh] = s_val

        for i in range(nch):
            sl = slice(i * c, (i + 1) * c)
            o = _dot(a_mats[i], vs[i]) + _dot(qf[c][sl], starts[i])
            on = _rms(o, g)
            r = r_ref[sl, vcol].astype(F32)
            og_ref[sl, vcol] = (on * (r * jax.nn.sigmoid(r))).astype(og_ref.dtype)

    @pl.when(t == pl.num_programs(2) - 1)
    def _():
        sout_ref[0] = s_ref[...]


def _side_rows(steps, arrays):
    pr = 16
    while pr <= max(a.shape[0] for a in arrays):
        if all(a.shape[0] % pr == 0 for a in arrays) and sum(a.shape[0] // pr for a in arrays) <= steps:
            return pr
        pr *= 2
    return 0


def _gla(z, a_lr, wa2, ba, g, s0, *, batch, seq, heads, dk, dv, off_q, off_k, off_v, off_r, tc, chunk, hb,
         cast=()):
    m = z.shape[0]
    tc = _tile(seq, tc)
    chunk = min(chunk, tc)
    nt = seq // tc
    hb = min(hb, heads)
    nh = heads // hb
    wk, wv = hb * dk, hb * dv
    assert heads % hb == 0
    assert off_q % wk == 0 and off_k % wk == 0 and off_v % wv == 0 and off_r % wv == 0
    assert chunk % GLA_BLOCK == 0 and tc % chunk == 0

    def zspec(width, off):
        return pl.BlockSpec((tc, width), lambda b, h, t: (b * nt + t, off // width + h))

    side_specs, side_shapes, side_ranges = [], [], []
    if cast:
        pr = _side_rows(batch * nh * nt, cast)
        assert pr > 0
        start = 0
        for a in cast:
            n_k = a.shape[0] // pr

            def idx(b, h, t, start=start, n_k=n_k):
                return (jnp.clip((b * nh + h) * nt + t - start, 0, n_k - 1), 0)

            side_specs.append(pl.BlockSpec((pr, a.shape[1]), idx))
            side_shapes.append(jax.ShapeDtypeStruct(a.shape, BF16))
            side_ranges.append((start, start + n_k))
            start += n_k

    return pl.pallas_call(
        functools.partial(_gla_kernel, chunk=chunk, scale=float(dk) ** -0.5, dk=dk, dv=dv,
                          side_ranges=tuple(side_ranges)),
        grid=(batch, nh, nt),
        in_specs=[
            zspec(wk, off_q), zspec(wk, off_k), zspec(wv, off_v), zspec(wv, off_r),
            pl.BlockSpec((tc, LANES), lambda b, h, t: (b * nt + t, 0)),
            pl.BlockSpec((LANES, wk), lambda b, h, t: (0, h)),
            pl.BlockSpec((1, wk), lambda b, h, t: (0, h)),
            pl.BlockSpec((1, dv), lambda b, h, t: (0, 0)),
            pl.BlockSpec((1, hb, dk, dv), lambda b, h, t: (b, h, 0, 0)),
        ] + side_specs,
        out_specs=[
            pl.BlockSpec((tc, wv), lambda b, h, t: (b * nt + t, h)),
            pl.BlockSpec((1, hb, dk, dv), lambda b, h, t: (b, h, 0, 0)),
        ] + side_specs,
        out_shape=[jax.ShapeDtypeStruct((m, heads * dv), BF16),
                   jax.ShapeDtypeStruct((batch, heads, dk, dv), F32)] + side_shapes,
        scratch_shapes=[pltpu.VMEM((hb, dk, dv), F32)],
        compiler_params=_cparams("arbitrary", "arbitrary", "arbitrary"),
        name="gla",
    )(z, z, z, z, a_lr, wa2, ba, g, s0, *cast)


def _conv_kernel(cb_ref, cc_ref, ch_ref, buf_ref, w_ref, cg_ref, nb_ref, tail_ref):
    t = pl.program_id(1)

    @pl.when(t == 0)
    def _():
        tail_ref[0:2, :] = buf_ref[0]

    tt = cc_ref.shape[0]
    u = cc_ref[...].astype(F32) * ch_ref[...].astype(F32)
    rows = lax.broadcasted_iota(jnp.int32, u.shape, 0)
    p0 = tail_ref[0:1, :]
    p1 = tail_ref[1:2, :]
    u1 = jnp.where(rows == 0, p1, pltpu.roll(u, 1, 0))
    u2 = jnp.where(rows == 0, p0, jnp.where(rows == 1, p1, pltpu.roll(u, 2, 0)))
    w = w_ref[...]
    conv = u2 * w[0:1, :] + u1 * w[1:2, :] + u * w[2:3, :]
    cg_ref[...] = (cb_ref[...].astype(F32) * conv).astype(cg_ref.dtype)
    last2 = u[tt - 2:tt, :]
    tail_ref[0:2, :] = last2

    @pl.when(t == pl.num_programs(1) - 1)
    def _():
        nb_ref[0] = last2


def _conv(z, buf, w, *, batch, seq, cw, off_cb, tt):
    m = z.shape[0]
    tt = _tile(seq, tt)
    nt = seq // tt
    assert off_cb % cw == 0 and tt >= 2
    kk = buf.shape[1]

    def zspec(k):
        return pl.BlockSpec((tt, cw), lambda b, t: (b * nt + t, off_cb // cw + k))

    return pl.pallas_call(
        _conv_kernel,
        grid=(batch, nt),
        in_specs=[zspec(0), zspec(1), zspec(2),
                  pl.BlockSpec((1, kk, cw), lambda b, t: (b, 0, 0)),
                  pl.BlockSpec((kk + 1, cw), lambda b, t: (0, 0))],
        out_specs=[pl.BlockSpec((tt, cw), lambda b, t: (b * nt + t, 0)),
                   pl.BlockSpec((1, kk, cw), lambda b, t: (b, 0, 0))],
        out_shape=[jax.ShapeDtypeStruct((m, cw), BF16), jax.ShapeDtypeStruct((batch, kk, cw), F32)],
        scratch_shapes=[pltpu.VMEM((8, cw), F32)],
        compiler_params=_cparams("parallel", "arbitrary"),
        name="conv",
    )(z, z, z, buf, w)


def _xattn_kernel(q_ref, *refs, scale, heads, dh):
    if len(refs) == 3:
        ks = pltpu.einshape("mhd->hmd", refs[0][...].astype(BF16))
        vs = pltpu.einshape("mhd->hmd", refs[1][...].astype(BF16))
        k_of, v_of = (lambda h: ks[h]), (lambda h: vs[h])
    else:
        k_of = lambda h: refs[h][...].astype(BF16)
        v_of = lambda h: refs[heads + h][...].astype(BF16)
    o_ref = refs[-1]
    for h in range(heads):
        col = slice(h * dh, (h + 1) * dh)
        s = _dot_nt(q_ref[:, col], k_of(h)) * scale
        p = jnp.exp(s - jnp.max(s, axis=-1, keepdims=True))
        p = p / jnp.sum(p, axis=-1, keepdims=True)
        o_ref[:, col] = _dot(p.astype(BF16), v_of(h)).astype(o_ref.dtype)


def _xattn(z, mem_k, mem_v, *, batch, seq, heads, dh, off_q, voff, tt):
    m = z.shape[0]
    tt = _tile(seq, tt)
    nt = seq // tt
    nm = mem_k.shape[1]
    xw = heads * dh
    assert off_q % xw == 0
    if mem_k.ndim == 4:
        mem_specs = [pl.BlockSpec((None, nm, heads, dh), lambda b, t: (b, 0, 0, 0))] * 2
        mems = (mem_k, mem_v)
    else:
        def spec(h, off):
            return pl.BlockSpec((None, nm, dh), lambda b, t: (b, 0, off * heads + h))
        mem_specs = [spec(h, 0) for h in range(heads)] + [spec(h, voff) for h in range(heads)]
        mems = (mem_k,) * heads + (mem_v,) * heads
    return pl.pallas_call(
        functools.partial(_xattn_kernel, scale=float(dh) ** -0.5, heads=heads, dh=dh),
        grid=(batch, nt),
        in_specs=[pl.BlockSpec((tt, xw), lambda b, t: (b * nt + t, off_q // xw))] + mem_specs,
        out_specs=pl.BlockSpec((tt, xw), lambda b, t: (b * nt + t, 0)),
        out_shape=jax.ShapeDtypeStruct((m, xw), BF16),
        compiler_params=_cparams("parallel", "parallel"),
        name="xattn",
    )(z, *mems)


def _mix_kernel(og_ref, cg_ref, at_ref, m0_ref, m1_ref, m2_ref, bm_ref, wg_ref, wc_ref, wx_ref, o_ref):
    bm = bm_ref[...]
    yg = _dot(og_ref[...], wg_ref[...])
    yc = _dot(cg_ref[...], wc_ref[...])
    yx = _dot(at_ref[...], wx_ref[...])
    mixed = (jax.nn.sigmoid(m0_ref[...].astype(F32) + bm[0:1, :]) * yg
             + jax.nn.sigmoid(m1_ref[...].astype(F32) + bm[1:2, :]) * yc
             + jax.nn.sigmoid(m2_ref[...].astype(F32) + bm[2:3, :]) * yx)
    o_ref[...] = mixed.astype(o_ref.dtype)


def _mix(og, cg, at, z, bm, wg, wc, wx, *, off_m, tm, tn):
    m = og.shape[0]
    d = wg.shape[1]
    tm, tn = _tile(m, tm), _tile(d, tn)
    assert off_m % tn == 0
    nb = bm.shape[0]

    def res(a):
        return pl.BlockSpec((tm, a.shape[1]), lambda i, j: (i, 0))

    def wspec(a):
        return pl.BlockSpec((a.shape[0], tn), lambda i, j: (0, j))

    def mspec(k):
        return pl.BlockSpec((tm, tn), lambda i, j: (i, (off_m + k * d) // tn + j))

    return pl.pallas_call(
        _mix_kernel,
        grid=(m // tm, d // tn),
        in_specs=[res(og), res(cg), res(at), mspec(0), mspec(1), mspec(2),
                  pl.BlockSpec((nb, tn), lambda i, j: (0, j)),
                  wspec(wg), wspec(wc), wspec(wx)],
        out_specs=pl.BlockSpec((tm, tn), lambda i, j: (i, j)),
        out_shape=jax.ShapeDtypeStruct((m, d), BF16),
        compiler_params=_cparams("parallel", "arbitrary"),
        name="mix",
    )(og, cg, at, z, z, z, bm, wg, wc, wx)


def _outproj_kernel(a_ref, w_ref, x_ref, o_ref):
    o_ref[...] = x_ref[...] + _dot(a_ref[...], w_ref[...])


def _outproj(a, w, x, *, tm, tn):
    m, k = a.shape
    n = w.shape[1]
    tm, tn = _tile(m, tm), _tile(n, tn)
    return pl.pallas_call(
        _outproj_kernel,
        grid=(m // tm, n // tn),
        in_specs=[pl.BlockSpec((tm, k), lambda i, j: (i, 0)),
                  pl.BlockSpec((k, tn), lambda i, j: (0, j)),
                  pl.BlockSpec((tm, tn), lambda i, j: (i, j))],
        out_specs=pl.BlockSpec((tm, tn), lambda i, j: (i, j)),
        out_shape=jax.ShapeDtypeStruct((m, n), F32),
        compiler_params=_cparams("parallel", "arbitrary"),
        name="outproj",
    )(a, w, x)


def _ffn_kernel(x_hbm, g_ref, wgu_ref, wd_ref, gf_ref, o_hbm, acc_ref, hn_ref, ld_sem, st_sem, *,
                final_norm, prefetch_step):
    i, f = pl.program_id(0), pl.program_id(1)
    ni, nf = pl.num_programs(0), pl.num_programs(1)
    tm = acc_ref.shape[1]
    slot = lax.rem(i, 2)
    acc = acc_ref.at[slot]

    def x_copy(tile, s):
        return pltpu.make_async_copy(x_hbm.at[pl.ds(tile * tm, tm), :], acc_ref.at[s], ld_sem.at[s])

    def o_copy(tile, s):
        return pltpu.make_async_copy(acc_ref.at[s], o_hbm.at[pl.ds(tile * tm, tm), :], st_sem.at[s])

    @pl.when(f == 0)
    def _():
        @pl.when(i == 0)
        def _():
            x_copy(0, 0).start()

        x_copy(i, slot).wait()

        def init(sl, x, xn):
            hn_ref[sl, :] = xn.astype(BF16)

        _rms_rows(tm, lambda sl: acc[sl, :], init, g_ref[...])

    @pl.when((f == prefetch_step) & (i + 1 < ni))
    def _():
        @pl.when(i >= 1)
        def _():
            o_copy(i - 1, 1 - slot).wait()

        x_copy(i + 1, 1 - slot).start()

    tf = wd_ref.shape[0]
    gu = _dot(hn_ref[...], wgu_ref[...])
    gate, up = gu[:, :tf], gu[:, tf:]
    h = ((gate * jax.nn.sigmoid(gate)) * up).astype(BF16)
    nb = min(FFN_NB, acc.shape[1])
    for n in range(acc.shape[1] // nb):
        cols = slice(n * nb, (n + 1) * nb)
        acc[:, cols] += _dot(h, wd_ref[:, cols])

    @pl.when(f == nf - 1)
    def _():
        if final_norm:
            def fin(sl, a, y):
                acc[sl, :] = y

            _rms_rows(tm, lambda sl: acc[sl, :], fin, gf_ref[...])
        o_copy(i, slot).start()

        @pl.when(i == ni - 1)
        def _():
            o_copy(i, slot).wait()

            @pl.when(i >= 1)
            def _():
                o_copy(i - 1, 1 - slot).wait()


def _ffn(x, g, wgu, wd, gf, *, final_norm, tm):
    m, d = x.shape
    nf, _, tf2 = wgu.shape
    tf = tf2 // 2
    tm = _tile(m, tm)
    return pl.pallas_call(
        functools.partial(_ffn_kernel, final_norm=final_norm, prefetch_step=min(FFN_PREFETCH_STEP, nf - 1)),
        grid=(m // tm, nf),
        in_specs=[pl.BlockSpec(memory_space=pl.ANY),
                  pl.BlockSpec((1, d), lambda i, f: (0, 0)),
                  pl.BlockSpec((None, d, tf2), lambda i, f: (f, 0, 0)),
                  pl.BlockSpec((tf, d), lambda i, f: (f, 0)),
                  pl.BlockSpec((1, d), lambda i, f: (0, 0))],
        out_specs=pl.BlockSpec(memory_space=pl.ANY),
        out_shape=jax.ShapeDtypeStruct((m, d), F32),
        scratch_shapes=[pltpu.VMEM((2, tm, d), F32), pltpu.VMEM((tm, d), BF16),
                        pltpu.SemaphoreType.DMA((2,)), pltpu.SemaphoreType.DMA((2,))],
        compiler_params=_cparams("arbitrary", "arbitrary"),
        name="ffn",
    )(x, g, wgu, wd, gf)


GLA_CHUNK = 128
GLA_BODIES = 8
FFN_TM = 1024
FFN_TF = 256
FFN_NB = 1024
FFN_PREFETCH_STEP = 8
INPROJ_TM = 1024
PROJ_WEIGHTS = ("w_gla_o", "w_conv_o", "w_xa_o", "w_out")


def _layer(x, z, a_lr, batch, seq, s0, buf, mem_k, mem_v, voff, p, dims, *, final_norm):
    d, gk, gv, cw, xw, heads, dk, dv, xh, dh = dims
    off_q, off_k, off_v, off_r = 0, gk, 2 * gk, 2 * gk + gv
    off_cb = off_r + gv
    off_xq = off_cb + 3 * cw
    off_m = off_xq + xw
    tc = _tile(seq, GLA_CHUNK * GLA_BODIES)
    hb = max(1, GLA_BODIES // max(1, tc // GLA_CHUNK))
    names = [k for k in PROJ_WEIGHTS if p[k].dtype != BF16]
    steps = batch * (heads // min(hb, heads)) * (seq // tc)
    if names and not _side_rows(steps, [p[k] for k in names]):
        for k in names:
            p[k] = p[k].astype(BF16)
        names = []
    og, s_new, *cast = _gla(z, a_lr, p["w_a2"], p["b_a"], p["g_gla_out"], s0, batch=batch, seq=seq, heads=heads,
                            dk=dk, dv=dv, off_q=off_q, off_k=off_k, off_v=off_v, off_r=off_r,
                            tc=tc, chunk=GLA_CHUNK, hb=hb, cast=tuple(p[k] for k in names))
    p.update(zip(names, cast))
    cg, new_buf = _conv(z, buf, p["w_conv"], batch=batch, seq=seq, cw=cw, off_cb=off_cb, tt=512)
    at = _xattn(z, mem_k, mem_v, batch=batch, seq=seq, heads=xh, dh=dh, off_q=off_xq, voff=voff, tt=1024)
    mixed = _mix(og, cg, at, z, p["b_merge"], p["w_gla_o"], p["w_conv_o"], p["w_xa_o"],
                 off_m=off_m, tm=1024, tn=1024)
    x1 = _outproj(mixed, p["w_out"], x, tm=1024, tn=1024)
    x2 = _ffn(x1, p["g_ffn"], p["w_ffn_gu"], p["w_ffn_down"], p["g_final"], final_norm=final_norm, tm=FFN_TM)
    return x2, s_new, new_buf


def kernel(x_prompt, x_sample, state_gla, cache_conv, cache_mem_k, cache_mem_v, mem_prompt, g_mix, w_in, w_a2, b_a, g_gla_out, w_gla_o, w_conv, w_conv_o, w_xa_o, g_mem, w_mem_kv, b_merge, w_out, g_ffn, w_ffn_gate, w_ffn_up, w_ffn_down, g_final):
    depth, bs, heads, dk, dv = state_gla.shape
    bp, tp, d = x_prompt.shape
    _, ts, _ = x_sample.shape
    gk, gv = heads * dk, heads * dv
    cw = cache_conv.shape[-1]
    _, _, n_mem, xh, dh = cache_mem_k.shape
    xw = xh * dh
    rank = w_a2.shape[1]
    dims = (d, gk, gv, cw, xw, heads, dk, dv, xh, dh)
    off_a = 2 * gk + gv
    assert rank <= LANES
    tf = _tile(w_ffn_gate.shape[-1], FFN_TF)

    hp = x_prompt.reshape(bp * tp, d)
    hs = x_sample.reshape(bs * ts, d)
    row = lambda a: a.reshape(1, -1)
    outs = [[] for _ in range(6)]
    for l in range(depth):
        wt = jnp.transpose(w_in[l])
        p = {
            "g_mix": row(g_mix[l]),
            "w_main": _drop_rows_t(wt, off_a, rank, tn=1024, tk=1024),
            "w_alr": _rows_t_pad(wt, off_a, rank, tk=1024),
            "w_a2": jnp.pad(w_a2[l], ((0, LANES - rank), (0, 0))).astype(BF16),
            "b_a": row(b_a[l]),
            "g_gla_out": row(g_gla_out[l]),
            "w_gla_o": w_gla_o[l],
            "w_conv": w_conv[l],
            "w_conv_o": w_conv_o[l],
            "w_xa_o": w_xa_o[l],
            "b_merge": b_merge[l],
            "w_out": w_out[l],
            "g_ffn": row(g_ffn[l]),
            "g_final": row(g_final),
        }
        last = l == depth - 1
        ffn_w = (w_ffn_gate[l], w_ffn_up[l], w_ffn_down[l])
        nf = ffn_w[0].shape[1] // tf
        steps = (hp.shape[0] // _tile(hp.shape[0], INPROJ_TM)) * p["w_main"].shape[0]
        if _side_pieces(steps, nf, d, tf):
            zp, ap, p["w_ffn_gu"], p["w_ffn_down"] = _inproj(hp, p["g_mix"], p["w_main"], p["w_alr"],
                                                             tm=INPROJ_TM, ffn_w=ffn_w, tf=tf)
        else:
            zp, ap = _inproj(hp, p["g_mix"], p["w_main"], p["w_alr"], tm=INPROJ_TM)
            p["w_ffn_gu"] = _pair_tiles(ffn_w[0], ffn_w[1], tf)
            p["w_ffn_down"] = ffn_w[2].astype(BF16)
        zs, as_ = _inproj(hs, p["g_mix"], p["w_main"], p["w_alr"], tm=INPROJ_TM)
        kv = _norm_matmul(mem_prompt.reshape(bp * n_mem, d), row(g_mem[l]), w_mem_kv[l].astype(BF16),
                          tm=512, tn=512).reshape(bp, n_mem, 2 * xw)
        s0 = jnp.zeros((bp, heads, dk, dv), state_gla.dtype)
        buf0 = jnp.zeros((bp,) + cache_conv.shape[2:], cache_conv.dtype)
        hp, sp, bufp = _layer(hp, zp, ap, bp, tp, s0, buf0, kv, kv, 1, p, dims, final_norm=last)
        hs, ss, bufs = _layer(hs, zs, as_, bs, ts, state_gla[l], cache_conv[l], cache_mem_k[l], cache_mem_v[l],
                              0, p, dims, final_norm=last)
        new = (sp, bufp, kv[..., :xw].reshape(bp, n_mem, xh, dh), kv[..., xw:].reshape(bp, n_mem, xh, dh),
               ss, bufs)
        for acc, val in zip(outs, new):
            acc.append(val)
    return (hp.reshape(bp, tp, d), hs.reshape(bs, ts, d)) + tuple(jnp.stack(o) for o in outs)
```

```python
import functools

import jax
import jax.numpy as jnp
from jax import lax
from jax.experimental import pallas as pl
from jax.experimental.pallas import tpu as pltpu

F32 = jnp.float32
BF16 = jnp.bfloat16

EPS = 1e-6
GLA_BLOCK = 16
GATE_NORM = 16.0
LANES = 128
V7X_VMEM_BYTES = 64 * 1024 * 1024
VMEM_LIMIT = V7X_VMEM_BYTES - 4 * 1024 * 1024


def _cparams(*sem):
    return pltpu.CompilerParams(dimension_semantics=sem, vmem_limit_bytes=VMEM_LIMIT)


def _tile(n, pref):
    if n <= pref:
        return n
    t = pref
    while n % t:
        t //= 2
    return t


def _rms(x, g):
    ms = jnp.mean(x * x, axis=-1, keepdims=True)
    return x * lax.rsqrt(ms + EPS) * g


def _dot(a, b):
    return jnp.dot(a, b, preferred_element_type=F32)


def _dot_nt(a, b):
    return lax.dot_general(a, b, (((1,), (1,)), ((), ())), preferred_element_type=F32)


def _dot_tn(a, b):
    return lax.dot_general(a, b, (((0,), (0,)), ((), ())), preferred_element_type=F32)


ROW_BLOCK = 16
ROW_UNROLL = 8


def _rows_loop(n_rows, load, store):
    rb = min(ROW_BLOCK, n_rows)
    trips = n_rows // rb
    u = ROW_UNROLL if trips % ROW_UNROLL == 0 else 1

    def body(i, c):
        sls = [pl.ds(pl.multiple_of((i * u + k) * rb, rb), rb) for k in range(u)]
        vals = [load(sl) for sl in sls]
        for sl, v in zip(sls, vals):
            store(sl, v)
        return c

    lax.fori_loop(0, trips // u, body, 0)


def _rms_rows(n_rows, load, emit, g):
    _rows_loop(n_rows, load, lambda sl, x: emit(sl, x, _rms(x, g)))


def _drop_rows_t_kernel(a_ref, b_ref, o_ref, *, n_plain, shift):
    j = pl.program_id(0)

    @pl.when(j < n_plain)
    def _():
        o_ref[...] = a_ref[...].T.astype(BF16)

    @pl.when(j >= n_plain)
    def _():
        rows = jnp.concatenate([a_ref[shift:, :], b_ref[...]], axis=0)
        o_ref[...] = rows.T.astype(BF16)


def _drop_rows_t(wt, start, width, *, tn, tk):
    rows, cols = wt.shape
    n_out = rows - width
    tn, tk = _tile(n_out, tn), _tile(cols, tk)
    assert start % tn == 0 and width % 8 == 0 and tn % width == 0
    return pl.pallas_call(
        functools.partial(_drop_rows_t_kernel, n_plain=start // tn, shift=width),
        grid=(n_out // tn, cols // tk),
        in_specs=[pl.BlockSpec((tn, tk), lambda j, k: (j, k)),
                  pl.BlockSpec((width, tk), lambda j, k: ((j + 1) * (tn // width), k))],
        out_specs=pl.BlockSpec((None, tk, tn), lambda j, k: (j, k, 0)),
        out_shape=jax.ShapeDtypeStruct((n_out // tn, cols, tn), BF16),
        compiler_params=_cparams("parallel", "parallel"),
        name="wprep",
    )(wt, wt)


def _rows_t_pad_kernel(a_ref, o_ref):
    a = a_ref[...]
    full = jnp.concatenate([a, jnp.zeros((LANES - a.shape[0], a.shape[1]), a.dtype)], axis=0)
    o_ref[...] = full.T.astype(BF16)


def _rows_t_pad(wt, start, width, *, tk):
    cols = wt.shape[1]
    tk = _tile(cols, tk)
    assert start % width == 0 and width % 8 == 0 and width <= LANES
    return pl.pallas_call(
        _rows_t_pad_kernel,
        grid=(cols // tk,),
        in_specs=[pl.BlockSpec((width, tk), lambda k: (start // width, k))],
        out_specs=pl.BlockSpec((tk, LANES), lambda k: (k, 0)),
        out_shape=jax.ShapeDtypeStruct((cols, LANES), BF16),
        compiler_params=_cparams("parallel"),
        name="wprep_alr",
    )(wt)


def _pair_tiles_kernel(a_ref, b_ref, o_ref):
    tn = a_ref.shape[1]
    o_ref[:, :tn] = a_ref[...].astype(BF16)
    o_ref[:, tn:] = b_ref[...].astype(BF16)


def _pair_tiles(a, b, tn):
    k, n = a.shape
    assert a.shape == b.shape and n % tn == 0
    return pl.pallas_call(
        _pair_tiles_kernel,
        grid=(n // tn,),
        in_specs=[pl.BlockSpec((k, tn), lambda j: (0, j))] * 2,
        out_specs=pl.BlockSpec((None, k, 2 * tn), lambda j: (j, 0, 0)),
        out_shape=jax.ShapeDtypeStruct((n // tn, k, 2 * tn), BF16),
        compiler_params=_cparams("parallel"),
        name="wprep_ffn",
    )(a, b)


def _inproj_kernel(x_hbm, g_ref, w_ref, wa_ref, *rest):
    if len(rest) == 5:
        side = None
        z_ref, a_ref, xn_ref, x_ref, sem = rest
    else:
        side = rest[:3] + rest[5:7]
        z_ref, a_ref, xn_ref, x_ref, sem = rest[3], rest[4], rest[7], rest[8], rest[9]
    i = pl.program_id(0)
    tm = x_ref.shape[0]

    def x_copy(tile):
        return pltpu.make_async_copy(x_hbm.at[pl.ds(tile * tm, tm), :], x_ref, sem.at[0])

    @pl.when(pl.program_id(1) == 0)
    def _():
        @pl.when(i == 0)
        def _():
            x_copy(0).start()

        x_copy(i).wait()

        def emit(sl, x, xn):
            xn_ref[sl, :] = xn.astype(BF16)

        _rms_rows(tm, lambda sl: x_ref[sl, :], emit, g_ref[...])

        @pl.when(i + 1 < pl.num_programs(0))
        def _():
            x_copy(i + 1).start()

        a_ref[...] = _dot(xn_ref[...], wa_ref[...])

    z_ref[...] = _dot(xn_ref[...], w_ref[...]).astype(z_ref.dtype)

    if side is not None:
        cg_ref, cu_ref, cd_ref, ogu_ref, od_ref = side
        tf = cg_ref.shape[1]
        ogu_ref[:, :tf] = cg_ref[...].astype(BF16)
        ogu_ref[:, tf:] = cu_ref[...].astype(BF16)
        od_ref[...] = cd_ref[...].astype(BF16)


def _side_pieces(steps, nf, d, tf):
    for r in (8, 4, 2, 1):
        if nf * r <= steps and d % (8 * r) == 0 and tf % (16 * r) == 0:
            return r
    return 0


def _inproj(x, g, w, wa, *, tm, ffn_w=None, tf=None):
    m, d = x.shape
    nt, _, tn = w.shape
    n = nt * tn
    tm = _tile(m, tm)
    in_specs = [
        pl.BlockSpec(memory_space=pl.ANY),
        pl.BlockSpec((1, d), lambda i, j: (0, 0)),
        pl.BlockSpec((None, d, tn), lambda i, j: (j, 0, 0)),
        pl.BlockSpec((d, LANES), lambda i, j: (0, 0)),
    ]
    out_specs = [
        pl.BlockSpec((tm, tn), lambda i, j: (i, j)),
        pl.BlockSpec((tm, LANES), lambda i, j: (i, 0)),
    ]
    out_shape = [jax.ShapeDtypeStruct((m, n), BF16), jax.ShapeDtypeStruct((m, LANES), F32)]
    operands = [x, g, w, wa]
    if ffn_w is not None:
        dff = ffn_w[0].shape[1]
        nf = dff // tf
        r = _side_pieces((m // tm) * nt, nf, d, tf)
        assert r > 0
        last = nf * r - 1

        def piece(i, j):
            return jnp.minimum(i * nt + j, last)

        in_specs += [pl.BlockSpec((d // r, tf), lambda i, j: (piece(i, j) % r, piece(i, j) // r))] * 2
        in_specs += [pl.BlockSpec((tf // r, d), lambda i, j: (piece(i, j), 0))]
        out_specs += [pl.BlockSpec((None, d // r, 2 * tf), lambda i, j: (piece(i, j) // r, piece(i, j) % r, 0)),
                      pl.BlockSpec((tf // r, d), lambda i, j: (piece(i, j), 0))]
        out_shape += [jax.ShapeDtypeStruct((nf, d, 2 * tf), BF16), jax.ShapeDtypeStruct((dff, d), BF16)]
        operands += list(ffn_w)
    return pl.pallas_call(
        _inproj_kernel,
        grid=(m // tm, nt),
        in_specs=in_specs,
        out_specs=out_specs,
        out_shape=out_shape,
        scratch_shapes=[pltpu.VMEM((tm, d), BF16), pltpu.VMEM((tm, d), F32), pltpu.SemaphoreType.DMA((1,))],
        compiler_params=_cparams("arbitrary", "arbitrary"),
        name="inproj",
    )(*operands)


def _norm_matmul_kernel(x_ref, g_ref, w_ref, o_ref, xn_ref):
    @pl.when(pl.program_id(1) == 0)
    def _():
        def emit(sl, x, xn):
            xn_ref[sl, :] = xn.astype(BF16)

        _rms_rows(x_ref.shape[0], lambda sl: x_ref[sl, :], emit, g_ref[...])

    o_ref[...] = _dot(xn_ref[...], w_ref[...])


def _norm_matmul(x, g, w, *, tm, tn):
    m, d = x.shape
    n = w.shape[1]
    tm, tn = _tile(m, tm), _tile(n, tn)
    return pl.pallas_call(
        _norm_matmul_kernel,
        grid=(m // tm, n // tn),
        in_specs=[
            pl.BlockSpec((tm, d), lambda i, j: (i, 0)),
            pl.BlockSpec((1, d), lambda i, j: (0, 0)),
            pl.BlockSpec((d, tn), lambda i, j: (0, j)),
        ],
        out_specs=pl.BlockSpec((tm, tn), lambda i, j: (i, j)),
        out_shape=jax.ShapeDtypeStruct((m, n), F32),
        scratch_shapes=[pltpu.VMEM((tm, d), BF16)],
        compiler_params=_cparams("parallel", "arbitrary"),
        name="memkv",
    )(x, g, w)


def _log_sigmoid(x):
    return jnp.minimum(x, 0.0) - jnp.log(1.0 + jnp.exp(-jnp.abs(x)))


def _chunk_cumsum(x, pos, c):
    s = 1
    while s < c:
        x = x + jnp.where(pos >= s, pltpu.roll(x, s, 0), 0.0)
        s *= 2
    return x


def _group_end(x, h):
    n, w = x.shape
    return jnp.concatenate(
        [jnp.broadcast_to(x[(g + 1) * h - 1:(g + 1) * h, :], (h, w)) for g in range(n // h)], axis=0)


def _gla_kernel(q_ref, k_ref, v_ref, r_ref, a_ref, wa2_ref, ba_ref, g_ref, s0_ref, *rest,
                chunk, scale, dk, dv, side_ranges):
    n_side = len(side_ranges)
    side_in, (og_ref, sout_ref) = rest[:n_side], rest[n_side:n_side + 2]
    side_out, s_ref = rest[n_side + 2:2 * n_side + 2], rest[2 * n_side + 2]
    _gla_body(q_ref, k_ref, v_ref, r_ref, a_ref, wa2_ref, ba_ref, g_ref, s0_ref, og_ref, sout_ref, s_ref,
              chunk=chunk, scale=scale, dk=dk, dv=dv)
    step = (pl.program_id(0) * pl.num_programs(1) + pl.program_id(1)) * pl.num_programs(2) + pl.program_id(2)
    for src, dst, (lo, hi) in zip(side_in, side_out, side_ranges):
        @pl.when((step >= lo) & (step < hi))
        def _(src=src, dst=dst):
            dst[...] = src[...].astype(BF16)


def _gla_body(q_ref, k_ref, v_ref, r_ref, a_ref, wa2_ref, ba_ref, g_ref, s0_ref,
              og_ref, sout_ref, s_ref, *, chunk, scale, dk, dv):
    t = pl.program_id(2)

    @pl.when(t == 0)
    def _():
        s_ref[...] = s0_ref[0]

    tc = q_ref.shape[0]
    hb = s_ref.shape[0]
    c = chunk
    nch = tc // c
    pos = lax.broadcasted_iota(jnp.int32, (tc, dk), 0) & (c - 1)
    ri = lax.broadcasted_iota(jnp.int32, (c, c), 0)
    ci = lax.broadcasted_iota(jnp.int32, (c, c), 1)
    hs = []
    h = GLA_BLOCK
    while h <= c:
        hs.append(h)
        h *= 2
    blk = GLA_BLOCK.bit_length() - 1
    mask0 = (lax.shift_right_logical(ri, blk) == lax.shift_right_logical(ci, blk)) & (ci <= ri)
    masks = {}
    for h in hs[:-1]:
        sh = (2 * h).bit_length() - 1
        same = lax.shift_right_logical(ri, sh) == lax.shift_right_logical(ci, sh)
        masks[h] = same & ((ri & (2 * h - 1)) >= h) & ((ci & (2 * h - 1)) < h)

    g = g_ref[...]
    a_bf = a_ref[...].astype(BF16)

    for hh in range(hb):
        kcol = slice(hh * dk, (hh + 1) * dk)
        vcol = slice(hh * dv, (hh + 1) * dv)
        la = _log_sigmoid(_dot(a_bf, wa2_ref[:, kcol]) + ba_ref[:, kcol]) / GATE_NORM
        beta = _chunk_cumsum(la, pos, c)
        q = q_ref[:, kcol].astype(F32) * scale
        k = k_ref[:, kcol].astype(F32)
        qf, kf = {}, {}
        for h in hs:
            e_h = _group_end(beta, h)
            if h == c:
                s_h = jnp.zeros_like(beta)
                e_c = e_h
            else:
                s_h = jnp.where(pos >= h, pltpu.roll(e_h, h, 0), 0.0)
            qf[h] = (q * jnp.exp(beta - s_h)).astype(BF16)
            kf[h] = (k * jnp.exp(e_h - beta)).astype(BF16)
            if h == GLA_BLOCK:
                k_in = (k * jnp.exp(s_h - beta)).astype(BF16)
        dec_rows = jnp.exp(e_c)

        a_mats, incs, vs = [], [], []
        for i in range(nch):
            sl = slice(i * c, (i + 1) * c)
            a_mat = jnp.where(mask0, _dot_nt(qf[GLA_BLOCK][sl], k_in[sl]), 0.0)
            for h in hs[:-1]:
                a_mat = a_mat + jnp.where(masks[h], _dot_nt(qf[h][sl], kf[h][sl]), 0.0)
            v = v_ref[sl, vcol]
            a_mats.append(a_mat.astype(BF16))
            vs.append(v)
            incs.append(_dot_tn(kf[c][sl], v))

        s_val = s_ref[hh]
        starts = []
        for i in range(nch):
            starts.append(s_val.astype(BF16))
            dec = jnp.transpose(dec_rows[i * c:i * c + 8, :])[:, 0:1]
            s_val = s_val * dec + incs[i]
        s_ref[hh] = s_val

        for i in range(nch):
            sl = slice(i * c, (i + 1) * c)
            o = _dot(a_mats[i], vs[i]) + _dot(qf[c][sl], starts[i])
            on = _rms(o, g)
            r = r_ref[sl, vcol].astype(F32)
            og_ref[sl, vcol] = (on * (r * jax.nn.sigmoid(r))).astype(og_ref.dtype)

    @pl.when(t == pl.num_programs(2) - 1)
    def _():
        sout_ref[0] = s_ref[...]


def _side_rows(steps, arrays):
    pr = 16
    while pr <= max(a.shape[0] for a in arrays):
        if all(a.shape[0] % pr == 0 for a in arrays) and sum(a.shape[0] // pr for a in arrays) <= steps:
            return pr
        pr *= 2
    return 0


def _gla(z, a_lr, wa2, ba, g, s0, *, batch, seq, heads, dk, dv, off_q, off_k, off_v, off_r, tc, chunk, hb,
         cast=()):
    m = z.shape[0]
    tc = _tile(seq, tc)
    chunk = min(chunk, tc)
    nt = seq // tc
    hb = min(hb, heads)
    nh = heads // hb
    wk, wv = hb * dk, hb * dv
    assert heads % hb == 0
    assert off_q % wk == 0 and off_k % wk == 0 and off_v % wv == 0 and off_r % wv == 0
    assert chunk % GLA_BLOCK == 0 and tc % chunk == 0

    def zspec(width, off):
        return pl.BlockSpec((tc, width), lambda b, h, t: (b * nt + t, off // width + h))

    side_specs, side_shapes, side_ranges = [], [], []
    if cast:
        pr = _side_rows(batch * nh * nt, cast)
        assert pr > 0
        start = 0
        for a in cast:
            n_k = a.shape[0] // pr

            def idx(b, h, t, start=start, n_k=n_k):
                return (jnp.clip((b * nh + h) * nt + t - start, 0, n_k - 1), 0)

            side_specs.append(pl.BlockSpec((pr, a.shape[1]), idx))
            side_shapes.append(jax.ShapeDtypeStruct(a.shape, BF16))
            side_ranges.append((start, start + n_k))
            start += n_k

    return pl.pallas_call(
        functools.partial(_gla_kernel, chunk=chunk, scale=float(dk) ** -0.5, dk=dk, dv=dv,
                          side_ranges=tuple(side_ranges)),
        grid=(batch, nh, nt),
        in_specs=[
            zspec(wk, off_q), zspec(wk, off_k), zspec(wv, off_v), zspec(wv, off_r),
            pl.BlockSpec((tc, LANES), lambda b, h, t: (b * nt + t, 0)),
            pl.BlockSpec((LANES, wk), lambda b, h, t: (0, h)),
            pl.BlockSpec((1, wk), lambda b, h, t: (0, h)),
            pl.BlockSpec((1, dv), lambda b, h, t: (0, 0)),
            pl.BlockSpec((1, hb, dk, dv), lambda b, h, t: (b, h, 0, 0)),
        ] + side_specs,
        out_specs=[
            pl.BlockSpec((tc, wv), lambda b, h, t: (b * nt + t, h)),
            pl.BlockSpec((1, hb, dk, dv), lambda b, h, t: (b, h, 0, 0)),
        ] + side_specs,
        out_shape=[jax.ShapeDtypeStruct((m, heads * dv), BF16),
                   jax.ShapeDtypeStruct((batch, heads, dk, dv), F32)] + side_shapes,
        scratch_shapes=[pltpu.VMEM((hb, dk, dv), F32)],
        compiler_params=_cparams("arbitrary", "arbitrary", "arbitrary"),
        name="gla",
    )(z, z, z, z, a_lr, wa2, ba, g, s0, *cast)


def _conv_kernel(cb_ref, cc_ref, ch_ref, buf_ref, w_ref, cg_ref, nb_ref, tail_ref):
    t = pl.program_id(1)

    @pl.when(t == 0)
    def _():
        tail_ref[0:2, :] = buf_ref[0]

    tt = cc_ref.shape[0]
    u = cc_ref[...].astype(F32) * ch_ref[...].astype(F32)
    rows = lax.broadcasted_iota(jnp.int32, u.shape, 0)
    p0 = tail_ref[0:1, :]
    p1 = tail_ref[1:2, :]
    u1 = jnp.where(rows == 0, p1, pltpu.roll(u, 1, 0))
    u2 = jnp.where(rows == 0, p0, jnp.where(rows == 1, p1, pltpu.roll(u, 2, 0)))
    w = w_ref[...]
    conv = u2 * w[0:1, :] + u1 * w[1:2, :] + u * w[2:3, :]
    cg_ref[...] = (cb_ref[...].astype(F32) * conv).astype(cg_ref.dtype)
    last2 = u[tt - 2:tt, :]
    tail_ref[0:2, :] = last2

    @pl.when(t == pl.num_programs(1) - 1)
    def _():
        nb_ref[0] = last2


def _conv(z, buf, w, *, batch, seq, cw, off_cb, tt):
    m = z.shape[0]
    tt = _tile(seq, tt)
    nt = seq // tt
    assert off_cb % cw == 0 and tt >= 2
    kk = buf.shape[1]

    def zspec(k):
        return pl.BlockSpec((tt, cw), lambda b, t: (b * nt + t, off_cb // cw + k))

    return pl.pallas_call(
        _conv_kernel,
        grid=(batch, nt),
        in_specs=[zspec(0), zspec(1), zspec(2),
                  pl.BlockSpec((1, kk, cw), lambda b, t: (b, 0, 0)),
                  pl.BlockSpec((kk + 1, cw), lambda b, t: (0, 0))],
        out_specs=[pl.BlockSpec((tt, cw), lambda b, t: (b * nt + t, 0)),
                   pl.BlockSpec((1, kk, cw), lambda b, t: (b, 0, 0))],
        out_shape=[jax.ShapeDtypeStruct((m, cw), BF16), jax.ShapeDtypeStruct((batch, kk, cw), F32)],
        scratch_shapes=[pltpu.VMEM((8, cw), F32)],
        compiler_params=_cparams("parallel", "arbitrary"),
        name="conv",
    )(z, z, z, buf, w)


def _xattn_kernel(q_ref, *refs, scale, heads, dh):
    if len(refs) == 3:
        ks = pltpu.einshape("mhd->hmd", refs[0][...].astype(BF16))
        vs = pltpu.einshape("mhd->hmd", refs[1][...].astype(BF16))
        k_of, v_of = (lambda h: ks[h]), (lambda h: vs[h])
    else:
        k_of = lambda h: refs[h][...].astype(BF16)
        v_of = lambda h: refs[heads + h][...].astype(BF16)
    o_ref = refs[-1]
    for h in range(heads):
        col = slice(h * dh, (h + 1) * dh)
        s = _dot_nt(q_ref[:, col], k_of(h)) * scale
        p = jnp.exp(s - jnp.max(s, axis=-1, keepdims=True))
        p = p / jnp.sum(p, axis=-1, keepdims=True)
        o_ref[:, col] = _dot(p.astype(BF16), v_of(h)).astype(o_ref.dtype)


def _xattn(z, mem_k, mem_v, *, batch, seq, heads, dh, off_q, voff, tt):
    m = z.shape[0]
    tt = _tile(seq, tt)
    nt = seq // tt
    nm = mem_k.shape[1]
    xw = heads * dh
    assert off_q % xw == 0
    if mem_k.ndim == 4:
        mem_specs = [pl.BlockSpec((None, nm, heads, dh), lambda b, t: (b, 0, 0, 0))] * 2
        mems = (mem_k, mem_v)
    else:
        def spec(h, off):
            return pl.BlockSpec((None, nm, dh), lambda b, t: (b, 0, off * heads + h))
        mem_specs = [spec(h, 0) for h in range(heads)] + [spec(h, voff) for h in range(heads)]
        mems = (mem_k,) * heads + (mem_v,) * heads
    return pl.pallas_call(
        functools.partial(_xattn_kernel, scale=float(dh) ** -0.5, heads=heads, dh=dh),
        grid=(batch, nt),
        in_specs=[pl.BlockSpec((tt, xw), lambda b, t: (b * nt + t, off_q // xw))] + mem_specs,
        out_specs=pl.BlockSpec((tt, xw), lambda b, t: (b * nt + t, 0)),
        out_shape=jax.ShapeDtypeStruct((m, xw), BF16),
        compiler_params=_cparams("parallel", "parallel"),
        name="xattn",
    )(z, *mems)


def _mix_kernel(og_ref, cg_ref, at_ref, m0_ref, m1_ref, m2_ref, bm_ref, wg_ref, wc_ref, wx_ref, o_ref):
    bm = bm_ref[...]
    yg = _dot(og_ref[...], wg_ref[...])
    yc = _dot(cg_ref[...], wc_ref[...])
    yx = _dot(at_ref[...], wx_ref[...])
    mixed = (jax.nn.sigmoid(m0_ref[...].astype(F32) + bm[0:1, :]) * yg
             + jax.nn.sigmoid(m1_ref[...].astype(F32) + bm[1:2, :]) * yc
             + jax.nn.sigmoid(m2_ref[...].astype(F32) + bm[2:3, :]) * yx)
    o_ref[...] = mixed.astype(o_ref.dtype)


def _mix(og, cg, at, z, bm, wg, wc, wx, *, off_m, tm, tn):
    m = og.shape[0]
    d = wg.shape[1]
    tm, tn = _tile(m, tm), _tile(d, tn)
    assert off_m % tn == 0
    nb = bm.shape[0]

    def res(a):
        return pl.BlockSpec((tm, a.shape[1]), lambda i, j: (i, 0))

    def wspec(a):
        return pl.BlockSpec((a.shape[0], tn), lambda i, j: (0, j))

    def mspec(k):
        return pl.BlockSpec((tm, tn), lambda i, j: (i, (off_m + k * d) // tn + j))

    return pl.pallas_call(
        _mix_kernel,
        grid=(m // tm, d // tn),
        in_specs=[res(og), res(cg), res(at), mspec(0), mspec(1), mspec(2),
                  pl.BlockSpec((nb, tn), lambda i, j: (0, j)),
                  wspec(wg), wspec(wc), wspec(wx)],
        out_specs=pl.BlockSpec((tm, tn), lambda i, j: (i, j)),
        out_shape=jax.ShapeDtypeStruct((m, d), BF16),
        compiler_params=_cparams("parallel", "arbitrary"),
        name="mix",
    )(og, cg, at, z, z, z, bm, wg, wc, wx)


def _outproj_kernel(a_ref, w_ref, x_ref, o_ref):
    o_ref[...] = x_ref[...] + _dot(a_ref[...], w_ref[...])


def _outproj(a, w, x, *, tm, tn):
    m, k = a.shape
    n = w.shape[1]
    tm, tn = _tile(m, tm), _tile(n, tn)
    return pl.pallas_call(
        _outproj_kernel,
        grid=(m // tm, n // tn),
        in_specs=[pl.BlockSpec((tm, k), lambda i, j: (i, 0)),
                  pl.BlockSpec((k, tn), lambda i, j: (0, j)),
                  pl.BlockSpec((tm, tn), lambda i, j: (i, j))],
        out_specs=pl.BlockSpec((tm, tn), lambda i, j: (i, j)),
        out_shape=jax.ShapeDtypeStruct((m, n), F32),
        compiler_params=_cparams("parallel", "arbitrary"),
        name="outproj",
    )(a, w, x)


def _ffn_kernel(x_hbm, g_ref, wgu_ref, wd_ref, gf_ref, o_hbm, acc_ref, hn_ref, ld_sem, st_sem, *,
                final_norm, prefetch_step):
    i, f = pl.program_id(0), pl.program_id(1)
    ni, nf = pl.num_programs(0), pl.num_programs(1)
    tm = acc_ref.shape[1]
    slot = lax.rem(i, 2)
    acc = acc_ref.at[slot]

    def x_copy(tile, s):
        return pltpu.make_async_copy(x_hbm.at[pl.ds(tile * tm, tm), :], acc_ref.at[s], ld_sem.at[s])

    def o_copy(tile, s):
        return pltpu.make_async_copy(acc_ref.at[s], o_hbm.at[pl.ds(tile * tm, tm), :], st_sem.at[s])

    @pl.when(f == 0)
    def _():
        @pl.when(i == 0)
        def _():
            x_copy(0, 0).start()

        x_copy(i, slot).wait()

        def init(sl, x, xn):
            hn_ref[sl, :] = xn.astype(BF16)

        _rms_rows(tm, lambda sl: acc[sl, :], init, g_ref[...])

    @pl.when((f == prefetch_step) & (i + 1 < ni))
    def _():
        @pl.when(i >= 1)
        def _():
            o_copy(i - 1, 1 - slot).wait()

        x_copy(i + 1, 1 - slot).start()

    tf = wd_ref.shape[0]
    gu = _dot(hn_ref[...], wgu_ref[...])
    gate, up = gu[:, :tf], gu[:, tf:]
    h = ((gate * jax.nn.sigmoid(gate)) * up).astype(BF16)
    nb = min(FFN_NB, acc.shape[1])
    for n in range(acc.shape[1] // nb):
        cols = slice(n * nb, (n + 1) * nb)
        acc[:, cols] += _dot(h, wd_ref[:, cols])

    @pl.when(f == nf - 1)
    def _():
        if final_norm:
            def fin(sl, a, y):
                acc[sl, :] = y

            _rms_rows(tm, lambda sl: acc[sl, :], fin, gf_ref[...])
        o_copy(i, slot).start()

        @pl.when(i == ni - 1)
        def _():
            o_copy(i, slot).wait()

            @pl.when(i >= 1)
            def _():
                o_copy(i - 1, 1 - slot).wait()


def _ffn(x, g, wgu, wd, gf, *, final_norm, tm):
    m, d = x.shape
    nf, _, tf2 = wgu.shape
    tf = tf2 // 2
    tm = _tile(m, tm)
    return pl.pallas_call(
        functools.partial(_ffn_kernel, final_norm=final_norm, prefetch_step=min(FFN_PREFETCH_STEP, nf - 1)),
        grid=(m // tm, nf),
        in_specs=[pl.BlockSpec(memory_space=pl.ANY),
                  pl.BlockSpec((1, d), lambda i, f: (0, 0)),
                  pl.BlockSpec((None, d, tf2), lambda i, f: (f, 0, 0)),
                  pl.BlockSpec((tf, d), lambda i, f: (f, 0)),
                  pl.BlockSpec((1, d), lambda i, f: (0, 0))],
        out_specs=pl.BlockSpec(memory_space=pl.ANY),
        out_shape=jax.ShapeDtypeStruct((m, d), F32),
        scratch_shapes=[pltpu.VMEM((2, tm, d), F32), pltpu.VMEM((tm, d), BF16),
                        pltpu.SemaphoreType.DMA((2,)), pltpu.SemaphoreType.DMA((2,))],
        compiler_params=_cparams("arbitrary", "arbitrary"),
        name="ffn",
    )(x, g, wgu, wd, gf)


GLA_CHUNK = 128
GLA_BODIES = 8
FFN_TM = 1024
FFN_TF = 256
FFN_NB = 1024
FFN_PREFETCH_STEP = 8
INPROJ_TM = 1024
PROJ_WEIGHTS = ("w_gla_o", "w_conv_o", "w_xa_o", "w_out")


def _layer(x, z, a_lr, batch, seq, s0, buf, mem_k, mem_v, voff, p, dims, *, final_norm):
    d, gk, gv, cw, xw, heads, dk, dv, xh, dh = dims
    off_q, off_k, off_v, off_r = 0, gk, 2 * gk, 2 * gk + gv
    off_cb = off_r + gv
    off_xq = off_cb + 3 * cw
    off_m = off_xq + xw
    tc = _tile(seq, GLA_CHUNK * GLA_BODIES)
    hb = max(1, GLA_BODIES // max(1, tc // GLA_CHUNK))
    names = [k for k in PROJ_WEIGHTS if p[k].dtype != BF16]
    steps = batch * (heads // min(hb, heads)) * (seq // tc)
    if names and not _side_rows(steps, [p[k] for k in names]):
        for k in names:
            p[k] = p[k].astype(BF16)
        names = []
    og, s_new, *cast = _gla(z, a_lr, p["w_a2"], p["b_a"], p["g_gla_out"], s0, batch=batch, seq=seq, heads=heads,
                            dk=dk, dv=dv, off_q=off_q, off_k=off_k, off_v=off_v, off_r=off_r,
                            tc=tc, chunk=GLA_CHUNK, hb=hb, cast=tuple(p[k] for k in names))
    p.update(zip(names, cast))
    cg, new_buf = _conv(z, buf, p["w_conv"], batch=batch, seq=seq, cw=cw, off_cb=off_cb, tt=512)
    at = _xattn(z, mem_k, mem_v, batch=batch, seq=seq, heads=xh, dh=dh, off_q=off_xq, voff=voff, tt=1024)
    mixed = _mix(og, cg, at, z, p["b_merge"], p["w_gla_o"], p["w_conv_o"], p["w_xa_o"],
                 off_m=off_m, tm=1024, tn=1024)
    x1 = _outproj(mixed, p["w_out"], x, tm=1024, tn=1024)
    x2 = _ffn(x1, p["g_ffn"], p["w_ffn_gu"], p["w_ffn_down"], p["g_final"], final_norm=final_norm, tm=FFN_TM)
    return x2, s_new, new_buf


def kernel(x_prompt, x_sample, state_gla, cache_conv, cache_mem_k, cache_mem_v, mem_prompt, g_mix, w_in, w_a2, b_a, g_gla_out, w_gla_o, w_conv, w_conv_o, w_xa_o, g_mem, w_mem_kv, b_merge, w_out, g_ffn, w_ffn_gate, w_ffn_up, w_ffn_down, g_final):
    depth, bs, heads, dk, dv = state_gla.shape
    bp, tp, d = x_prompt.shape
    _, ts, _ = x_sample.shape
    gk, gv = heads * dk, heads * dv
    cw = cache_conv.shape[-1]
    _, _, n_mem, xh, dh = cache_mem_k.shape
    xw = xh * dh
    rank = w_a2.shape[1]
    dims = (d, gk, gv, cw, xw, heads, dk, dv, xh, dh)
    off_a = 2 * gk + gv
    assert rank <= LANES
    tf = _tile(w_ffn_gate.shape[-1], FFN_TF)

    hp = x_prompt.reshape(bp * tp, d)
    hs = x_sample.reshape(bs * ts, d)
    row = lambda a: a.reshape(1, -1)
    outs = [[] for _ in range(6)]
    for l in range(depth):
        wt = jnp.transpose(w_in[l])
        p = {
            "g_mix": row(g_mix[l]),
            "w_main": _drop_rows_t(wt, off_a, rank, tn=1024, tk=1024),
            "w_alr": _rows_t_pad(wt, off_a, rank, tk=1024),
            "w_a2": jnp.pad(w_a2[l], ((0, LANES - rank), (0, 0))).astype(BF16),
            "b_a": row(b_a[l]),
            "g_gla_out": row(g_gla_out[l]),
            "w_gla_o": w_gla_o[l],
            "w_conv": w_conv[l],
            "w_conv_o": w_conv_o[l],
            "w_xa_o": w_xa_o[l],
            "b_merge": b_merge[l],
            "w_out": w_out[l],
            "g_ffn": row(g_ffn[l]),
            "g_final": row(g_final),
        }
        last = l == depth - 1
        ffn_w = (w_ffn_gate[l], w_ffn_up[l], w_ffn_down[l])
        nf = ffn_w[0].shape[1] // tf
        steps = (hp.shape[0] // _tile(hp.shape[0], INPROJ_TM)) * p["w_main"].shape[0]
        if _side_pieces(steps, nf, d, tf):
            zp, ap, p["w_ffn_gu"], p["w_ffn_down"] = _inproj(hp, p["g_mix"], p["w_main"], p["w_alr"],
                                                             tm=INPROJ_TM, ffn_w=ffn_w, tf=tf)
        else:
            zp, ap = _inproj(hp, p["g_mix"], p["w_main"], p["w_alr"], tm=INPROJ_TM)
            p["w_ffn_gu"] = _pair_tiles(ffn_w[0], ffn_w[1], tf)
            p["w_ffn_down"] = ffn_w[2].astype(BF16)
        zs, as_ = _inproj(hs, p["g_mix"], p["w_main"], p["w_alr"], tm=INPROJ_TM)
        kv = _norm_matmul(mem_prompt.reshape(bp * n_mem, d), row(g_mem[l]), w_mem_kv[l].astype(BF16),
                          tm=512, tn=512).reshape(bp, n_mem, 2 * xw)
        s0 = jnp.zeros((bp, heads, dk, dv), state_gla.dtype)
        buf0 = jnp.zeros((bp,) + cache_conv.shape[2:], cache_conv.dtype)
        hp, sp, bufp = _layer(hp, zp, ap, bp, tp, s0, buf0, kv, kv, 1, p, dims, final_norm=last)
        hs, ss, bufs = _layer(hs, zs, as_, bs, ts, state_gla[l], cache_conv[l], cache_mem_k[l], cache_mem_v[l],
                              0, p, dims, final_norm=last)
        new = (sp, bufp, kv[..., :xw].reshape(bp, n_mem, xh, dh), kv[..., xw:].reshape(bp, n_mem, xh, dh),
               ss, bufs)
        for acc, val in zip(outs, new):
            acc.append(val)
    return (hp.reshape(bp, tp, d), hs.reshape(bs, ts, d)) + tuple(jnp.stack(o) for o in outs)
```

```python
import functools

import jax
import jax.numpy as jnp
from jax import lax
from jax.experimental import pallas as pl
from jax.experimental.pallas import tpu as pltpu

F32 = jnp.float32
BF16 = jnp.bfloat16

EPS = 1e-6
GLA_BLOCK = 16
GATE_NORM = 16.0
LANES = 128
V7X_VMEM_BYTES = 64 * 1024 * 1024
VMEM_LIMIT = V7X_VMEM_BYTES - 4 * 1024 * 1024


def _cparams(*sem):
    return pltpu.CompilerParams(dimension_semantics=sem, vmem_limit_bytes=VMEM_LIMIT)


def _tile(n, pref):
    if n <= pref:
        return n
    t = pref
    while n % t:
        t //= 2
    return t


def _rms(x, g):
    ms = jnp.mean(x * x, axis=-1, keepdims=True)
    return x * lax.rsqrt(ms + EPS) * g


def _dot(a, b):
    return jnp.dot(a, b, preferred_element_type=F32)


def _dot_nt(a, b):
    return lax.dot_general(a, b, (((1,), (1,)), ((), ())), preferred_element_type=F32)


def _dot_tn(a, b):
    return lax.dot_general(a, b, (((0,), (0,)), ((), ())), preferred_element_type=F32)


ROW_BLOCK = 16
ROW_UNROLL = 8


def _rows_loop(n_rows, load, store):
    rb = min(ROW_BLOCK, n_rows)
    trips = n_rows // rb
    u = ROW_UNROLL if trips % ROW_UNROLL == 0 else 1

    def body(i, c):
        sls = [pl.ds(pl.multiple_of((i * u + k) * rb, rb), rb) for k in range(u)]
        vals = [load(sl) for sl in sls]
        for sl, v in zip(sls, vals):
            store(sl, v)
        return c

    lax.fori_loop(0, trips // u, body, 0)


def _rms_rows(n_rows, load, emit, g):
    _rows_loop(n_rows, load, lambda sl, x: emit(sl, x, _rms(x, g)))


def _drop_rows_t_kernel(a_ref, b_ref, o_ref, *, n_plain, shift):
    j = pl.program_id(0)

    @pl.when(j < n_plain)
    def _():
        o_ref[...] = a_ref[...].T.astype(BF16)

    @pl.when(j >= n_plain)
    def _():
        rows = jnp.concatenate([a_ref[shift:, :], b_ref[...]], axis=0)
        o_ref[...] = rows.T.astype(BF16)


def _drop_rows_t(wt, start, width, *, tn, tk):
    rows, cols = wt.shape
    n_out = rows - width
    tn, tk = _tile(n_out, tn), _tile(cols, tk)
    assert start % tn == 0 and width % 8 == 0 and tn % width == 0
    return pl.pallas_call(
        functools.partial(_drop_rows_t_kernel, n_plain=start // tn, shift=width),
        grid=(n_out // tn, cols // tk),
        in_specs=[pl.BlockSpec((tn, tk), lambda j, k: (j, k)),
                  pl.BlockSpec((width, tk), lambda j, k: ((j + 1) * (tn // width), k))],
        out_specs=pl.BlockSpec((None, tk, tn), lambda j, k: (j, k, 0)),
        out_shape=jax.ShapeDtypeStruct((n_out // tn, cols, tn), BF16),
        compiler_params=_cparams("parallel", "parallel"),
        name="wprep",
    )(wt, wt)


def _rows_t_pad_kernel(a_ref, o_ref):
    a = a_ref[...]
    full = jnp.concatenate([a, jnp.zeros((LANES - a.shape[0], a.shape[1]), a.dtype)], axis=0)
    o_ref[...] = full.T.astype(BF16)


def _rows_t_pad(wt, start, width, *, tk):
    cols = wt.shape[1]
    tk = _tile(cols, tk)
    assert start % width == 0 and width % 8 == 0 and width <= LANES
    return pl.pallas_call(
        _rows_t_pad_kernel,
        grid=(cols // tk,),
        in_specs=[pl.BlockSpec((width, tk), lambda k: (start // width, k))],
        out_specs=pl.BlockSpec((tk, LANES), lambda k: (k, 0)),
        out_shape=jax.ShapeDtypeStruct((cols, LANES), BF16),
        compiler_params=_cparams("parallel"),
        name="wprep_alr",
    )(wt)


def _pair_tiles_kernel(a_ref, b_ref, o_ref):
    tn = a_ref.shape[1]
    o_ref[:, :tn] = a_ref[...].astype(BF16)
    o_ref[:, tn:] = b_ref[...].astype(BF16)


def _pair_tiles(a, b, tn):
    k, n = a.shape
    assert a.shape == b.shape and n % tn == 0
    return pl.pallas_call(
        _pair_tiles_kernel,
        grid=(n // tn,),
        in_specs=[pl.BlockSpec((k, tn), lambda j: (0, j))] * 2,
        out_specs=pl.BlockSpec((None, k, 2 * tn), lambda j: (j, 0, 0)),
        out_shape=jax.ShapeDtypeStruct((n // tn, k, 2 * tn), BF16),
        compiler_params=_cparams("parallel"),
        name="wprep_ffn",
    )(a, b)


def _inproj_kernel(x_hbm, g_ref, w_ref, wa_ref, *rest):
    if len(rest) == 5:
        side = None
        z_ref, a_ref, xn_ref, x_ref, sem = rest
    else:
        side = rest[:3] + rest[5:7]
        z_ref, a_ref, xn_ref, x_ref, sem = rest[3], rest[4], rest[7], rest[8], rest[9]
    i = pl.program_id(0)
    tm = x_ref.shape[0]

    def x_copy(tile):
        return pltpu.make_async_copy(x_hbm.at[pl.ds(tile * tm, tm), :], x_ref, sem.at[0])

    @pl.when(pl.program_id(1) == 0)
    def _():
        @pl.when(i == 0)
        def _():
            x_copy(0).start()

        x_copy(i).wait()

        def emit(sl, x, xn):
            xn_ref[sl, :] = xn.astype(BF16)

        _rms_rows(tm, lambda sl: x_ref[sl, :], emit, g_ref[...])

        @pl.when(i + 1 < pl.num_programs(0))
        def _():
            x_copy(i + 1).start()

        a_ref[...] = _dot(xn_ref[...], wa_ref[...])

    z_ref[...] = _dot(xn_ref[...], w_ref[...]).astype(z_ref.dtype)

    if side is not None:
        cg_ref, cu_ref, cd_ref, ogu_ref, od_ref = side
        tf = cg_ref.shape[1]
        ogu_ref[:, :tf] = cg_ref[...].astype(BF16)
        ogu_ref[:, tf:] = cu_ref[...].astype(BF16)
        od_ref[...] = cd_ref[...].astype(BF16)


def _side_pieces(steps, nf, d, tf):
    for r in (8, 4, 2, 1):
        if nf * r <= steps and d % (8 * r) == 0 and tf % (16 * r) == 0:
            return r
    return 0


def _inproj(x, g, w, wa, *, tm, ffn_w=None, tf=None):
    m, d = x.shape
    nt, _, tn = w.shape
    n = nt * tn
    tm = _tile(m, tm)
    in_specs = [
        pl.BlockSpec(memory_space=pl.ANY),
        pl.BlockSpec((1, d), lambda i, j: (0, 0)),
        pl.BlockSpec((None, d, tn), lambda i, j: (j, 0, 0)),
        pl.BlockSpec((d, LANES), lambda i, j: (0, 0)),
    ]
    out_specs = [
        pl.BlockSpec((tm, tn), lambda i, j: (i, j)),
        pl.BlockSpec((tm, LANES), lambda i, j: (i, 0)),
    ]
    out_shape = [jax.ShapeDtypeStruct((m, n), BF16), jax.ShapeDtypeStruct((m, LANES), F32)]
    operands = [x, g, w, wa]
    if ffn_w is not None:
        dff = ffn_w[0].shape[1]
        nf = dff // tf
        r = _side_pieces((m // tm) * nt, nf, d, tf)
        assert r > 0
        last = nf * r - 1

        def piece(i, j):
            return jnp.minimum(i * nt + j, last)

        in_specs += [pl.BlockSpec((d // r, tf), lambda i, j: (piece(i, j) % r, piece(i, j) // r))] * 2
        in_specs += [pl.BlockSpec((tf // r, d), lambda i, j: (piece(i, j), 0))]
        out_specs += [pl.BlockSpec((None, d // r, 2 * tf), lambda i, j: (piece(i, j) // r, piece(i, j) % r, 0)),
                      pl.BlockSpec((tf // r, d), lambda i, j: (piece(i, j), 0))]
        out_shape += [jax.ShapeDtypeStruct((nf, d, 2 * tf), BF16), jax.ShapeDtypeStruct((dff, d), BF16)]
        operands += list(ffn_w)
    return pl.pallas_call(
        _inproj_kernel,
        grid=(m // tm, nt),
        in_specs=in_specs,
        out_specs=out_specs,
        out_shape=out_shape,
        scratch_shapes=[pltpu.VMEM((tm, d), BF16), pltpu.VMEM((tm, d), F32), pltpu.SemaphoreType.DMA((1,))],
        compiler_params=_cparams("arbitrary", "arbitrary"),
        name="inproj",
    )(*operands)


def _norm_matmul_kernel(x_ref, g_ref, w_ref, o_ref, xn_ref):
    @pl.when(pl.program_id(1) == 0)
    def _():
        def emit(sl, x, xn):
            xn_ref[sl, :] = xn.astype(BF16)

        _rms_rows(x_ref.shape[0], lambda sl: x_ref[sl, :], emit, g_ref[...])

    o_ref[...] = _dot(xn_ref[...], w_ref[...])


def _norm_matmul(x, g, w, *, tm, tn):
    m, d = x.shape
    n = w.shape[1]
    tm, tn = _tile(m, tm), _tile(n, tn)
    return pl.pallas_call(
        _norm_matmul_kernel,
        grid=(m // tm, n // tn),
        in_specs=[
            pl.BlockSpec((tm, d), lambda i, j: (i, 0)),
            pl.BlockSpec((1, d), lambda i, j: (0, 0)),
            pl.BlockSpec((d, tn), lambda i, j: (0, j)),
        ],
        out_specs=pl.BlockSpec((tm, tn), lambda i, j: (i, j)),
        out_shape=jax.ShapeDtypeStruct((m, n), F32),
        scratch_shapes=[pltpu.VMEM((tm, d), BF16)],
        compiler_params=_cparams("parallel", "arbitrary"),
        name="memkv",
    )(x, g, w)


def _log_sigmoid(x):
    return jnp.minimum(x, 0.0) - jnp.log(1.0 + jnp.exp(-jnp.abs(x)))


def _chunk_cumsum(x, pos, c):
    s = 1
    while s < c:
        x = x + jnp.where(pos >= s, pltpu.roll(x, s, 0), 0.0)
        s *= 2
    return x


def _group_end(x, h):
    n, w = x.shape
    return jnp.concatenate(
        [jnp.broadcast_to(x[(g + 1) * h - 1:(g + 1) * h, :], (h, w)) for g in range(n // h)], axis=0)


def _gla_kernel(q_ref, k_ref, v_ref, r_ref, a_ref, wa2_ref, ba_ref, g_ref, s0_ref, *rest,
                chunk, scale, dk, dv, side_ranges):
    n_side = len(side_ranges)
    side_in, (og_ref, sout_ref) = rest[:n_side], rest[n_side:n_side + 2]
    side_out, s_ref = rest[n_side + 2:2 * n_side + 2], rest[2 * n_side + 2]
    _gla_body(q_ref, k_ref, v_ref, r_ref, a_ref, wa2_ref, ba_ref, g_ref, s0_ref, og_ref, sout_ref, s_ref,
              chunk=chunk, scale=scale, dk=dk, dv=dv)
    step = (pl.program_id(0) * pl.num_programs(1) + pl.program_id(1)) * pl.num_programs(2) + pl.program_id(2)
    for src, dst, (lo, hi) in zip(side_in, side_out, side_ranges):
        @pl.when((step >= lo) & (step < hi))
        def _(src=src, dst=dst):
            dst[...] = src[...].astype(BF16)


def _gla_body(q_ref, k_ref, v_ref, r_ref, a_ref, wa2_ref, ba_ref, g_ref, s0_ref,
              og_ref, sout_ref, s_ref, *, chunk, scale, dk, dv):
    t = pl.program_id(2)

    @pl.when(t == 0)
    def _():
        s_ref[...] = s0_ref[0]

    tc = q_ref.shape[0]
    hb = s_ref.shape[0]
    c = chunk
    nch = tc // c
    pos = lax.broadcasted_iota(jnp.int32, (tc, dk), 0) & (c - 1)
    ri = lax.broadcasted_iota(jnp.int32, (c, c), 0)
    ci = lax.broadcasted_iota(jnp.int32, (c, c), 1)
    hs = []
    h = GLA_BLOCK
    while h <= c:
        hs.append(h)
        h *= 2
    blk = GLA_BLOCK.bit_length() - 1
    mask0 = (lax.shift_right_logical(ri, blk) == lax.shift_right_logical(ci, blk)) & (ci <= ri)
    masks = {}
    for h in hs[:-1]:
        sh = (2 * h).bit_length() - 1
        same = lax.shift_right_logical(ri, sh) == lax.shift_right_logical(ci, sh)
        masks[h] = same & ((ri & (2 * h - 1)) >= h) & ((ci & (2 * h - 1)) < h)

    g = g_ref[...]
    a_bf = a_ref[...].astype(BF16)

    for hh in range(hb):
        kcol = slice(hh * dk, (hh + 1) * dk)
        vcol = slice(hh * dv, (hh + 1) * dv)
        la = _log_sigmoid(_dot(a_bf, wa2_ref[:, kcol]) + ba_ref[:, kcol]) / GATE_NORM
        beta = _chunk_cumsum(la, pos, c)
        q = q_ref[:, kcol].astype(F32) * scale
        k = k_ref[:, kcol].astype(F32)
        qf, kf = {}, {}
        for h in hs:
            e_h = _group_end(beta, h)
            if h == c:
                s_h = jnp.zeros_like(beta)
                e_c = e_h
            else:
                s_h = jnp.where(pos >= h, pltpu.roll(e_h, h, 0), 0.0)
            qf[h] = (q * jnp.exp(beta - s_h)).astype(BF16)
            kf[h] = (k * jnp.exp(e_h - beta)).astype(BF16)
            if h == GLA_BLOCK:
                k_in = (k * jnp.exp(s_h - beta)).astype(BF16)
        dec_rows = jnp.exp(e_c)

        a_mats, incs, vs = [], [], []
        for i in range(nch):
            sl = slice(i * c, (i + 1) * c)
            a_mat = jnp.where(mask0, _dot_nt(qf[GLA_BLOCK][sl], k_in[sl]), 0.0)
            for h in hs[:-1]:
                a_mat = a_mat + jnp.where(masks[h], _dot_nt(qf[h][sl], kf[h][sl]), 0.0)
            v = v_ref[sl, vcol]
            a_mats.append(a_mat.astype(BF16))
            vs.append(v)
            incs.append(_dot_tn(kf[c][sl], v))

        s_val = s_ref[hh]
        starts = []
        for i in range(nch):
            starts.append(s_val.astype(BF16))
            dec = jnp.transpose(dec_rows[i * c:i * c + 8, :])[:, 0:1]
            s_val = s_val * dec + incs[i]
        s_ref[hh] = s_val

        for i in range(nch):
            sl = slice(i * c, (i + 1) * c)
            o = _dot(a_mats[i], vs[i]) + _dot(qf[c][sl], starts[i])
            on = _rms(o, g)
            r = r_ref[sl, vcol].astype(F32)
            og_ref[sl, vcol] = (on * (r * jax.nn.sigmoid(r))).astype(og_ref.dtype)

    @pl.when(t == pl.num_programs(2) - 1)
    def _():
        sout_ref[0] = s_ref[...]


def _side_rows(steps, arrays):
    pr = 16
    while pr <= max(a.shape[0] for a in arrays):
        if all(a.shape[0] % pr == 0 for a in arrays) and sum(a.shape[0] // pr for a in arrays) <= steps:
            return pr
        pr *= 2
    return 0


def _gla(z, a_lr, wa2, ba, g, s0, *, batch, seq, heads, dk, dv, off_q, off_k, off_v, off_r, tc, chunk, hb,
         cast=()):
    m = z.shape[0]
    tc = _tile(seq, tc)
    chunk = min(chunk, tc)
    nt = seq // tc
    hb = min(hb, heads)
    nh = heads // hb
    wk, wv = hb * dk, hb * dv
    assert heads % hb == 0
    assert off_q % wk == 0 and off_k % wk == 0 and off_v % wv == 0 and off_r % wv == 0
    assert chunk % GLA_BLOCK == 0 and tc % chunk == 0

    def zspec(width, off):
        return pl.BlockSpec((tc, width), lambda b, h, t: (b * nt + t, off // width + h))

    side_specs, side_shapes, side_ranges = [], [], []
    if cast:
        pr = _side_rows(batch * nh * nt, cast)
        assert pr > 0
        start = 0
        for a in cast:
            n_k = a.shape[0] // pr

            def idx(b, h, t, start=start, n_k=n_k):
                return (jnp.clip((b * nh + h) * nt + t - start, 0, n_k - 1), 0)

            side_specs.append(pl.BlockSpec((pr, a.shape[1]), idx))
            side_shapes.append(jax.ShapeDtypeStruct(a.shape, BF16))
            side_ranges.append((start, start + n_k))
            start += n_k

    return pl.pallas_call(
        functools.partial(_gla_kernel, chunk=chunk, scale=float(dk) ** -0.5, dk=dk, dv=dv,
                          side_ranges=tuple(side_ranges)),
        grid=(batch, nh, nt),
        in_specs=[
            zspec(wk, off_q), zspec(wk, off_k), zspec(wv, off_v), zspec(wv, off_r),
            pl.BlockSpec((tc, LANES), lambda b, h, t: (b * nt + t, 0)),
            pl.BlockSpec((LANES, wk), lambda b, h, t: (0, h)),
            pl.BlockSpec((1, wk), lambda b, h, t: (0, h)),
            pl.BlockSpec((1, dv), lambda b, h, t: (0, 0)),
            pl.BlockSpec((1, hb, dk, dv), lambda b, h, t: (b, h, 0, 0)),
        ] + side_specs,
        out_specs=[
            pl.BlockSpec((tc, wv), lambda b, h, t: (b * nt + t, h)),
            pl.BlockSpec((1, hb, dk, dv), lambda b, h, t: (b, h, 0, 0)),
        ] + side_specs,
        out_shape=[jax.ShapeDtypeStruct((m, heads * dv), BF16),
                   jax.ShapeDtypeStruct((batch, heads, dk, dv), F32)] + side_shapes,
        scratch_shapes=[pltpu.VMEM((hb, dk, dv), F32)],
        compiler_params=_cparams("arbitrary", "arbitrary", "arbitrary"),
        name="gla",
    )(z, z, z, z, a_lr, wa2, ba, g, s0, *cast)


def _conv_kernel(cb_ref, cc_ref, ch_ref, buf_ref, w_ref, cg_ref, nb_ref, tail_ref):
    t = pl.program_id(1)

    @pl.when(t == 0)
    def _():
        tail_ref[0:2, :] = buf_ref[0]

    tt = cc_ref.shape[0]
    u = cc_ref[...].astype(F32) * ch_ref[...].astype(F32)
    rows = lax.broadcasted_iota(jnp.int32, u.shape, 0)
    p0 = tail_ref[0:1, :]
    p1 = tail_ref[1:2, :]
    u1 = jnp.where(rows == 0, p1, pltpu.roll(u, 1, 0))
    u2 = jnp.where(rows == 0, p0, jnp.where(rows == 1, p1, pltpu.roll(u, 2, 0)))
    w = w_ref[...]
    conv = u2 * w[0:1, :] + u1 * w[1:2, :] + u * w[2:3, :]
    cg_ref[...] = (cb_ref[...].astype(F32) * conv).astype(cg_ref.dtype)
    last2 = u[tt - 2:tt, :]
    tail_ref[0:2, :] = last2

    @pl.when(t == pl.num_programs(1) - 1)
    def _():
        nb_ref[0] = last2


def _conv(z, buf, w, *, batch, seq, cw, off_cb, tt):
    m = z.shape[0]
    tt = _tile(seq, tt)
    nt = seq // tt
    assert off_cb % cw == 0 and tt >= 2
    kk = buf.shape[1]

    def zspec(k):
        return pl.BlockSpec((tt, cw), lambda b, t: (b * nt + t, off_cb // cw + k))

    return pl.pallas_call(
        _conv_kernel,
        grid=(batch, nt),
        in_specs=[zspec(0), zspec(1), zspec(2),
                  pl.BlockSpec((1, kk, cw), lambda b, t: (b, 0, 0)),
                  pl.BlockSpec((kk + 1, cw), lambda b, t: (0, 0))],
        out_specs=[pl.BlockSpec((tt, cw), lambda b, t: (b * nt + t, 0)),
                   pl.BlockSpec((1, kk, cw), lambda b, t: (b, 0, 0))],
        out_shape=[jax.ShapeDtypeStruct((m, cw), BF16), jax.ShapeDtypeStruct((batch, kk, cw), F32)],
        scratch_shapes=[pltpu.VMEM((8, cw), F32)],
        compiler_params=_cparams("parallel", "arbitrary"),
        name="conv",
    )(z, z, z, buf, w)


def _xattn_kernel(q_ref, *refs, scale, heads, dh):
    if len(refs) == 3:
        ks = pltpu.einshape("mhd->hmd", refs[0][...].astype(BF16))
        vs = pltpu.einshape("mhd->hmd", refs[1][...].astype(BF16))
        k_of, v_of = (lambda h: ks[h]), (lambda h: vs[h])
    else:
        k_of = lambda h: refs[h][...].astype(BF16)
        v_of = lambda h: refs[heads + h][...].astype(BF16)
    o_ref = refs[-1]
    for h in range(heads):
        col = slice(h * dh, (h + 1) * dh)
        s = _dot_nt(q_ref[:, col], k_of(h)) * scale
        p = jnp.exp(s - jnp.max(s, axis=-1, keepdims=True))
        p = p / jnp.sum(p, axis=-1, keepdims=True)
        o_ref[:, col] = _dot(p.astype(BF16), v_of(h)).astype(o_ref.dtype)


def _xattn(z, mem_k, mem_v, *, batch, seq, heads, dh, off_q, voff, tt):
    m = z.shape[0]
    tt = _tile(seq, tt)
    nt = seq // tt
    nm = mem_k.shape[1]
    xw = heads * dh
    assert off_q % xw == 0
    if mem_k.ndim == 4:
        mem_specs = [pl.BlockSpec((None, nm, heads, dh), lambda b, t: (b, 0, 0, 0))] * 2
        mems = (mem_k, mem_v)
    else:
        def spec(h, off):
            return pl.BlockSpec((None, nm, dh), lambda b, t: (b, 0, off * heads + h))
        mem_specs = [spec(h, 0) for h in range(heads)] + [spec(h, voff) for h in range(heads)]
        mems = (mem_k,) * heads + (mem_v,) * heads
    return pl.pallas_call(
        functools.partial(_xattn_kernel, scale=float(dh) ** -0.5, heads=heads, dh=dh),
        grid=(batch, nt),
        in_specs=[pl.BlockSpec((tt, xw), lambda b, t: (b * nt + t, off_q // xw))] + mem_specs,
        out_specs=pl.BlockSpec((tt, xw), lambda b, t: (b * nt + t, 0)),
        out_shape=jax.ShapeDtypeStruct((m, xw), BF16),
        compiler_params=_cparams("parallel", "parallel"),
        name="xattn",
    )(z, *mems)


def _conv_xattn_kernel(cb_ref, cc_ref, ch_ref, buf_ref, w_ref, q_ref, *rest, scale, heads, dh, n_mem):
    mem, (cg_ref, nb_ref, o_ref, tail_ref) = rest[:n_mem], rest[n_mem:]
    _conv_kernel(cb_ref, cc_ref, ch_ref, buf_ref, w_ref, cg_ref, nb_ref, tail_ref)
    _xattn_kernel(q_ref, *mem, o_ref, scale=scale, heads=heads, dh=dh)


def _conv_xattn(z, buf, w, mem_k, mem_v, *, batch, seq, cw, off_cb, heads, dh, off_q, voff, tt):
    m = z.shape[0]
    tt = _tile(seq, tt)
    nt = seq // tt
    nm = mem_k.shape[1]
    xw = heads * dh
    kk = buf.shape[1]
    assert off_cb % cw == 0 and tt >= 2 and off_q % xw == 0

    def zspec(k):
        return pl.BlockSpec((tt, cw), lambda b, t: (b * nt + t, off_cb // cw + k))

    if mem_k.ndim == 4:
        mem_specs = [pl.BlockSpec((None, nm, heads, dh), lambda b, t: (b, 0, 0, 0))] * 2
        mems = (mem_k, mem_v)
    else:
        def spec(h, off):
            return pl.BlockSpec((None, nm, dh), lambda b, t: (b, 0, off * heads + h))
        mem_specs = [spec(h, 0) for h in range(heads)] + [spec(h, voff) for h in range(heads)]
        mems = (mem_k,) * heads + (mem_v,) * heads
    return pl.pallas_call(
        functools.partial(_conv_xattn_kernel, scale=float(dh) ** -0.5, heads=heads, dh=dh, n_mem=len(mems)),
        grid=(batch, nt),
        in_specs=[zspec(0), zspec(1), zspec(2),
                  pl.BlockSpec((1, kk, cw), lambda b, t: (b, 0, 0)),
                  pl.BlockSpec((kk + 1, cw), lambda b, t: (0, 0)),
                  pl.BlockSpec((tt, xw), lambda b, t: (b * nt + t, off_q // xw))] + mem_specs,
        out_specs=[pl.BlockSpec((tt, cw), lambda b, t: (b * nt + t, 0)),
                   pl.BlockSpec((1, kk, cw), lambda b, t: (b, 0, 0)),
                   pl.BlockSpec((tt, xw), lambda b, t: (b * nt + t, 0))],
        out_shape=[jax.ShapeDtypeStruct((m, cw), BF16), jax.ShapeDtypeStruct((batch, kk, cw), F32),
                   jax.ShapeDtypeStruct((m, xw), BF16)],
        scratch_shapes=[pltpu.VMEM((8, cw), F32)],
        compiler_params=_cparams("parallel", "arbitrary"),
        name="conv_xattn",
    )(z, z, z, buf, w, z, *mems)


def _mix_kernel(og_ref, cg_ref, at_ref, m0_ref, m1_ref, m2_ref, bm_ref, wg_ref, wc_ref, wx_ref, o_ref):
    bm = bm_ref[...]
    yg = _dot(og_ref[...], wg_ref[...])
    yc = _dot(cg_ref[...], wc_ref[...])
    yx = _dot(at_ref[...], wx_ref[...])
    mixed = (jax.nn.sigmoid(m0_ref[...].astype(F32) + bm[0:1, :]) * yg
             + jax.nn.sigmoid(m1_ref[...].astype(F32) + bm[1:2, :]) * yc
             + jax.nn.sigmoid(m2_ref[...].astype(F32) + bm[2:3, :]) * yx)
    o_ref[...] = mixed.astype(o_ref.dtype)


def _mix(og, cg, at, z, bm, wg, wc, wx, *, off_m, tm, tn):
    m = og.shape[0]
    d = wg.shape[1]
    tm, tn = _tile(m, tm), _tile(d, tn)
    assert off_m % tn == 0
    nb = bm.shape[0]

    def res(a):
        return pl.BlockSpec((tm, a.shape[1]), lambda i, j: (i, 0))

    def wspec(a):
        return pl.BlockSpec((a.shape[0], tn), lambda i, j: (0, j))

    def mspec(k):
        return pl.BlockSpec((tm, tn), lambda i, j: (i, (off_m + k * d) // tn + j))

    return pl.pallas_call(
        _mix_kernel,
        grid=(m // tm, d // tn),
        in_specs=[res(og), res(cg), res(at), mspec(0), mspec(1), mspec(2),
                  pl.BlockSpec((nb, tn), lambda i, j: (0, j)),
                  wspec(wg), wspec(wc), wspec(wx)],
        out_specs=pl.BlockSpec((tm, tn), lambda i, j: (i, j)),
        out_shape=jax.ShapeDtypeStruct((m, d), BF16),
        compiler_params=_cparams("parallel", "arbitrary"),
        name="mix",
    )(og, cg, at, z, z, z, bm, wg, wc, wx)


def _outproj_kernel(a_ref, w_ref, x_ref, o_ref):
    o_ref[...] = x_ref[...] + _dot(a_ref[...], w_ref[...])


def _outproj(a, w, x, *, tm, tn):
    m, k = a.shape
    n = w.shape[1]
    tm, tn = _tile(m, tm), _tile(n, tn)
    return pl.pallas_call(
        _outproj_kernel,
        grid=(m // tm, n // tn),
        in_specs=[pl.BlockSpec((tm, k), lambda i, j: (i, 0)),
                  pl.BlockSpec((k, tn), lambda i, j: (0, j)),
                  pl.BlockSpec((tm, tn), lambda i, j: (i, j))],
        out_specs=pl.BlockSpec((tm, tn), lambda i, j: (i, j)),
        out_shape=jax.ShapeDtypeStruct((m, n), F32),
        compiler_params=_cparams("parallel", "arbitrary"),
        name="outproj",
    )(a, w, x)


def _ffn_kernel(x_hbm, g_ref, wgu_ref, wd_ref, gf_ref, o_hbm, acc_ref, hn_ref, ld_sem, st_sem, *,
                final_norm, prefetch_step):
    i, f = pl.program_id(0), pl.program_id(1)
    ni, nf = pl.num_programs(0), pl.num_programs(1)
    tm = acc_ref.shape[1]
    slot = lax.rem(i, 2)
    acc = acc_ref.at[slot]

    def x_copy(tile, s):
        return pltpu.make_async_copy(x_hbm.at[pl.ds(tile * tm, tm), :], acc_ref.at[s], ld_sem.at[s])

    def o_copy(tile, s):
        return pltpu.make_async_copy(acc_ref.at[s], o_hbm.at[pl.ds(tile * tm, tm), :], st_sem.at[s])

    @pl.when(f == 0)
    def _():
        @pl.when(i == 0)
        def _():
            x_copy(0, 0).start()

        x_copy(i, slot).wait()

        def init(sl, x, xn):
            hn_ref[sl, :] = xn.astype(BF16)

        _rms_rows(tm, lambda sl: acc[sl, :], init, g_ref[...])

    @pl.when((f == prefetch_step) & (i + 1 < ni))
    def _():
        @pl.when(i >= 1)
        def _():
            o_copy(i - 1, 1 - slot).wait()

        x_copy(i + 1, 1 - slot).start()

    tf = wd_ref.shape[0]
    gu = _dot(hn_ref[...], wgu_ref[...])
    gate, up = gu[:, :tf], gu[:, tf:]
    h = ((gate * jax.nn.sigmoid(gate)) * up).astype(BF16)
    nb = min(FFN_NB, acc.shape[1])
    for n in range(acc.shape[1] // nb):
        cols = slice(n * nb, (n + 1) * nb)
        acc[:, cols] += _dot(h, wd_ref[:, cols])

    @pl.when(f == nf - 1)
    def _():
        if final_norm:
            def fin(sl, a, y):
                acc[sl, :] = y

            _rms_rows(tm, lambda sl: acc[sl, :], fin, gf_ref[...])
        o_copy(i, slot).start()

        @pl.when(i == ni - 1)
        def _():
            o_copy(i, slot).wait()

            @pl.when(i >= 1)
            def _():
                o_copy(i - 1, 1 - slot).wait()


def _ffn(x, g, wgu, wd, gf, *, final_norm, tm):
    m, d = x.shape
    nf, _, tf2 = wgu.shape
    tf = tf2 // 2
    tm = _tile(m, tm)
    return pl.pallas_call(
        functools.partial(_ffn_kernel, final_norm=final_norm, prefetch_step=min(FFN_PREFETCH_STEP, nf - 1)),
        grid=(m // tm, nf),
        in_specs=[pl.BlockSpec(memory_space=pl.ANY),
                  pl.BlockSpec((1, d), lambda i, f: (0, 0)),
                  pl.BlockSpec((None, d, tf2), lambda i, f: (f, 0, 0)),
                  pl.BlockSpec((tf, d), lambda i, f: (f, 0)),
                  pl.BlockSpec((1, d), lambda i, f: (0, 0))],
        out_specs=pl.BlockSpec(memory_space=pl.ANY),
        out_shape=jax.ShapeDtypeStruct((m, d), F32),
        scratch_shapes=[pltpu.VMEM((2, tm, d), F32), pltpu.VMEM((tm, d), BF16),
                        pltpu.SemaphoreType.DMA((2,)), pltpu.SemaphoreType.DMA((2,))],
        compiler_params=_cparams("arbitrary", "arbitrary"),
        name="ffn",
    )(x, g, wgu, wd, gf)


GLA_CHUNK = 128
GLA_BODIES = 8
FFN_TM = 1024
FFN_TF = 256
FFN_NB = 1024
FFN_PREFETCH_STEP = 8
INPROJ_TM = 1024
PROJ_WEIGHTS = ("w_gla_o", "w_conv_o", "w_xa_o", "w_out")


def _layer(x, z, a_lr, batch, seq, s0, buf, mem_k, mem_v, voff, p, dims, *, final_norm):
    d, gk, gv, cw, xw, heads, dk, dv, xh, dh = dims
    off_q, off_k, off_v, off_r = 0, gk, 2 * gk, 2 * gk + gv
    off_cb = off_r + gv
    off_xq = off_cb + 3 * cw
    off_m = off_xq + xw
    tc = _tile(seq, GLA_CHUNK * GLA_BODIES)
    hb = max(1, GLA_BODIES // max(1, tc // GLA_CHUNK))
    names = [k for k in PROJ_WEIGHTS if p[k].dtype != BF16]
    steps = batch * (heads // min(hb, heads)) * (seq // tc)
    if names and not _side_rows(steps, [p[k] for k in names]):
        for k in names:
            p[k] = p[k].astype(BF16)
        names = []
    og, s_new, *cast = _gla(z, a_lr, p["w_a2"], p["b_a"], p["g_gla_out"], s0, batch=batch, seq=seq, heads=heads,
                            dk=dk, dv=dv, off_q=off_q, off_k=off_k, off_v=off_v, off_r=off_r,
                            tc=tc, chunk=GLA_CHUNK, hb=hb, cast=tuple(p[k] for k in names))
    p.update(zip(names, cast))
    cg, new_buf, at = _conv_xattn(z, buf, p["w_conv"], mem_k, mem_v, batch=batch, seq=seq, cw=cw, off_cb=off_cb,
                                  heads=xh, dh=dh, off_q=off_xq, voff=voff, tt=1024)
    mixed = _mix(og, cg, at, z, p["b_merge"], p["w_gla_o"], p["w_conv_o"], p["w_xa_o"],
                 off_m=off_m, tm=1024, tn=1024)
    x1 = _outproj(mixed, p["w_out"], x, tm=1024, tn=1024)
    x2 = _ffn(x1, p["g_ffn"], p["w_ffn_gu"], p["w_ffn_down"], p["g_final"], final_norm=final_norm, tm=FFN_TM)
    return x2, s_new, new_buf


def kernel(x_prompt, x_sample, state_gla, cache_conv, cache_mem_k, cache_mem_v, mem_prompt, g_mix, w_in, w_a2, b_a, g_gla_out, w_gla_o, w_conv, w_conv_o, w_xa_o, g_mem, w_mem_kv, b_merge, w_out, g_ffn, w_ffn_gate, w_ffn_up, w_ffn_down, g_final):
    depth, bs, heads, dk, dv = state_gla.shape
    bp, tp, d = x_prompt.shape
    _, ts, _ = x_sample.shape
    gk, gv = heads * dk, heads * dv
    cw = cache_conv.shape[-1]
    _, _, n_mem, xh, dh = cache_mem_k.shape
    xw = xh * dh
    rank = w_a2.shape[1]
    dims = (d, gk, gv, cw, xw, heads, dk, dv, xh, dh)
    off_a = 2 * gk + gv
    assert rank <= LANES
    tf = _tile(w_ffn_gate.shape[-1], FFN_TF)

    hp = x_prompt.reshape(bp * tp, d)
    hs = x_sample.reshape(bs * ts, d)
    row = lambda a: a.reshape(1, -1)
    outs = [[] for _ in range(6)]
    for l in range(depth):
        wt = jnp.transpose(w_in[l])
        p = {
            "g_mix": row(g_mix[l]),
            "w_main": _drop_rows_t(wt, off_a, rank, tn=1024, tk=1024),
            "w_alr": _rows_t_pad(wt, off_a, rank, tk=1024),
            "w_a2": jnp.pad(w_a2[l], ((0, LANES - rank), (0, 0))).astype(BF16),
            "b_a": row(b_a[l]),
            "g_gla_out": row(g_gla_out[l]),
            "w_gla_o": w_gla_o[l],
            "w_conv": w_conv[l],
            "w_conv_o": w_conv_o[l],
            "w_xa_o": w_xa_o[l],
            "b_merge": b_merge[l],
            "w_out": w_out[l],
            "g_ffn": row(g_ffn[l]),
            "g_final": row(g_final),
        }
        last = l == depth - 1
        ffn_w = (w_ffn_gate[l], w_ffn_up[l], w_ffn_down[l])
        nf = ffn_w[0].shape[1] // tf
        steps = (hp.shape[0] // _tile(hp.shape[0], INPROJ_TM)) * p["w_main"].shape[0]
        if _side_pieces(steps, nf, d, tf):
            zp, ap, p["w_ffn_gu"], p["w_ffn_down"] = _inproj(hp, p["g_mix"], p["w_main"], p["w_alr"],
                                                             tm=INPROJ_TM, ffn_w=ffn_w, tf=tf)
        else:
            zp, ap = _inproj(hp, p["g_mix"], p["w_main"], p["w_alr"], tm=INPROJ_TM)
            p["w_ffn_gu"] = _pair_tiles(ffn_w[0], ffn_w[1], tf)
            p["w_ffn_down"] = ffn_w[2].astype(BF16)
        zs, as_ = _inproj(hs, p["g_mix"], p["w_main"], p["w_alr"], tm=INPROJ_TM)
        kv = _norm_matmul(mem_prompt.reshape(bp * n_mem, d), row(g_mem[l]), w_mem_kv[l].astype(BF16),
                          tm=512, tn=512).reshape(bp, n_mem, 2 * xw)
        s0 = jnp.zeros((bp, heads, dk, dv), state_gla.dtype)
        buf0 = jnp.zeros((bp,) + cache_conv.shape[2:], cache_conv.dtype)
        hp, sp, bufp = _layer(hp, zp, ap, bp, tp, s0, buf0, kv, kv, 1, p, dims, final_norm=last)
        hs, ss, bufs = _layer(hs, zs, as_, bs, ts, state_gla[l], cache_conv[l], cache_mem_k[l], cache_mem_v[l],
                              0, p, dims, final_norm=last)
        new = (sp, bufp, kv[..., :xw].reshape(bp, n_mem, xh, dh), kv[..., xw:].reshape(bp, n_mem, xh, dh),
               ss, bufs)
        for acc, val in zip(outs, new):
            acc.append(val)
    return (hp.reshape(bp, tp, d), hs.reshape(bs, ts, d)) + tuple(jnp.stack(o) for o in outs)
```
